```python
import math
import jax, jax.numpy as jnp
from jax import lax
import numpy as np

D_MODEL = 1024
BATCH = 8
SEQ = 2048
DEPTH = 4

GRID_W = 64
CTX_LEN = 256
N_MIXERS = 2
N_A = (DEPTH + 1) // 2
N_B = DEPTH // 2
N_HEADS = 8
NOPE_DIM = 128
ROPE_DIM = 64
V_DIM = 128
Q_LORA = 384
KV_LORA = 256
ATTN_W = N_HEADS * V_DIM
MLA_IN = Q_LORA + KV_LORA + ROPE_DIM + ATTN_W
ROPE_BASE = 10000.0
Q_BLOCK = 128
CONV_W = D_MODEL
CONV_K = 31
CONV_PAD = CONV_K // 2
EPS = 1e-6

kernel_name = "hybrid_mla_conformer_prefix_dit"


def rms_norm(x, g):
    xf = x.astype(jnp.float32)
    y = xf * lax.rsqrt(jnp.mean(xf * xf, axis=-1, keepdims=True) + EPS)
    return (y * g.astype(jnp.float32)).astype(x.dtype)


def layer_norm(x, g, b):
    xf = x.astype(jnp.float32)
    mu = jnp.mean(xf, axis=-1, keepdims=True)
    var = jnp.mean(jnp.square(xf - mu), axis=-1, keepdims=True)
    y = (xf - mu) * lax.rsqrt(var + EPS)
    return (y * g.astype(jnp.float32) + b.astype(jnp.float32)).astype(x.dtype)


def axial_rope_tables(n_tokens, dtype):
    rows_n = n_tokens // GRID_W
    rows = jnp.repeat(jnp.arange(rows_n, dtype=jnp.float32), GRID_W)
    cols = jnp.tile(jnp.arange(GRID_W, dtype=jnp.float32), rows_n)
    half = ROPE_DIM // 2
    freqs = 1.0 / (ROPE_BASE ** (jnp.arange(0, half, 2, dtype=jnp.float32) / half))
    ang_r = rows[:, None] * freqs[None, :]
    ang_c = cols[:, None] * freqs[None, :]
    return tuple(t.astype(dtype) for t in (jnp.cos(ang_r), jnp.sin(ang_r), jnp.cos(ang_c), jnp.sin(ang_c)))


def _rotate(x, cos, sin):
    x1, x2 = jnp.split(x, 2, axis=-1)
    return jnp.concatenate([x1 * cos - x2 * sin, x1 * sin + x2 * cos], axis=-1)


def rope_2d(x, tables):
    cr, sr, cc, sc = tables
    xr, xc = jnp.split(x, 2, axis=-1)
    return jnp.concatenate([_rotate(xr, cr, sr), _rotate(xc, cc, sc)], axis=-1)


def mla_qkv(h, w_in, g_q, w_uq, g_kv, w_ukv, tables):
    b, l, _ = h.shape
    u = h @ w_in
    cq, ckv, kr, gate = jnp.split(u, [Q_LORA, Q_LORA + KV_LORA, Q_LORA + KV_LORA + ROPE_DIM], axis=-1)
    q = (rms_norm(cq, g_q) @ w_uq).reshape(b, l, N_HEADS, NOPE_DIM + ROPE_DIM).transpose(0, 2, 1, 3)
    kv = (rms_norm(ckv, g_kv) @ w_ukv).reshape(b, l, N_HEADS, NOPE_DIM + V_DIM).transpose(0, 2, 1, 3)
    k_nope, v = jnp.split(kv, [NOPE_DIM], axis=-1)
    q_nope, q_rope = jnp.split(q, [NOPE_DIM], axis=-1)
    if tables is not None:
        q_rope = rope_2d(q_rope, tables)
        kr = rope_2d(kr, tables)
    k_rope = jnp.broadcast_to(kr[:, None], (b, N_HEADS, l, ROPE_DIM))
    q = jnp.concatenate([q_nope, q_rope], axis=-1)
    k = jnp.concatenate([k_nope, k_rope], axis=-1)
    return q, k, v, gate


def block_attention(q, k, v):
    b, h, lq, dq = q.shape
    dv = v.shape[-1]
    nb = lq // Q_BLOCK
    scale = 1.0 / math.sqrt(dq)
    qb = q.reshape(b, h, nb, Q_BLOCK, dq).transpose(2, 0, 1, 3, 4)

    def one_block(qi):
        s = jnp.einsum('bhqd,bhkd->bhqk', qi, k).astype(jnp.float32) * scale
        p = jax.nn.softmax(s, axis=-1).astype(v.dtype)
        return jnp.einsum('bhqk,bhkd->bhqd', p, v)

    o = lax.map(one_block, qb)
    return o.transpose(1, 0, 3, 2, 4).reshape(b, lq, h * dv)


def conformer_conv_branch(h, w_in, w_dw, b_dw, g_ln, b_ln, w_o):
    a, bgl, g = jnp.split(h @ w_in, 3, axis=-1)
    u = a * jax.nn.sigmoid(bgl)
    u = lax.conv_general_dilated(u, w_dw, window_strides=(1,), padding=[(CONV_PAD, CONV_PAD)],
                                 dimension_numbers=('NWC', 'WIO', 'NWC'),
                                 feature_group_count=CONV_W) + b_dw
    u = jax.nn.silu(layer_norm(u, g_ln, b_ln)) * jax.nn.silu(g)
    return u @ w_o


def setup_inputs(seed: int = 0) -> dict:
    key = jax.random.key(seed)
    ks = iter(jax.random.split(key, 32))
    f32 = jnp.float32

    def nrm(shape, scale):
        return jax.random.normal(next(ks), shape, f32) * scale

    d = D_MODEL
    return {
        "x": nrm((BATCH, SEQ, d), 1.0),
        "c": nrm((BATCH, d), 1.0),
        "ctx": nrm((BATCH, CTX_LEN, d), 1.0),
        "c_ctx": nrm((d,), 1.0),
        "w_mod": nrm((DEPTH, d, 3 * d), 0.3 * d ** -0.5),
        "b_mod": nrm((DEPTH, 3 * d), 0.01),
        "g_pre": 1.0 + nrm((DEPTH, d), 0.02),
        "g_post": 1.0 + nrm((DEPTH, d), 0.02),
        "mla_w_in": nrm((N_A, d, MLA_IN), d ** -0.5),
        "mla_g_q": 1.0 + nrm((N_A, Q_LORA), 0.02),
        "mla_w_uq": nrm((N_A, Q_LORA, N_HEADS * (NOPE_DIM + ROPE_DIM)), Q_LORA ** -0.5),
        "mla_g_kv": 1.0 + nrm((N_A, KV_LORA), 0.02),
        "mla_w_ukv": nrm((N_A, KV_LORA, N_HEADS * (NOPE_DIM + V_DIM)), KV_LORA ** -0.5),
        "mla_w_o": nrm((N_A, ATTN_W, d), ATTN_W ** -0.5),
        "cv_w_in": nrm((N_B, d, 3 * CONV_W), d ** -0.5),
        "cv_w_dw": nrm((N_B, CONV_K, 1, CONV_W), CONV_K ** -0.5),
        "cv_b_dw": nrm((N_B, CONV_W), 0.01),
        "cv_g_ln": 1.0 + nrm((N_B, CONV_W), 0.02),
        "cv_b_ln": nrm((N_B, CONV_W), 0.01),
        "cv_w_o": nrm((N_B, CONV_W, d), CONV_W ** -0.5),
    }


def reference(x, c, ctx, c_ctx, w_mod, b_mod, g_pre, g_post,
              mla_w_in, mla_g_q, mla_w_uq, mla_g_kv, mla_w_ukv, mla_w_o,
              cv_w_in, cv_w_dw, cv_b_dw, cv_g_ln, cv_b_ln, cv_w_o):
    b, s, d = x.shape
    tables = axial_rope_tables(s, x.dtype)
    sc = jax.nn.silu(c)
    sc_ctx = jax.nn.silu(c_ctx)

    for i in range(DEPTH):
        last = i == DEPTH - 1
        j = i // N_MIXERS
        shift_x, scale_x, gate_x = jnp.split((sc @ w_mod[i] + b_mod[i])[:, None, :], 3, axis=-1)
        shift_c, scale_c, gate_c = jnp.split(sc_ctx @ w_mod[i] + b_mod[i], 3, axis=-1)
        hx = rms_norm(x, g_pre[i]) * (1.0 + scale_x) + shift_x
        hc = rms_norm(ctx, g_pre[i]) * (1.0 + scale_c) + shift_c

        if i % N_MIXERS == 0:
            qx, kx, vx, gx = mla_qkv(hx, mla_w_in[j], mla_g_q[j], mla_w_uq[j], mla_g_kv[j], mla_w_ukv[j], tables)
            qc, kc, vc, gc = mla_qkv(hc, mla_w_in[j], mla_g_q[j], mla_w_uq[j], mla_g_kv[j], mla_w_ukv[j], None)
            ox = block_attention(qx, jnp.concatenate([kx, kc], axis=2), jnp.concatenate([vx, vc], axis=2))
            yx = (ox * jax.nn.silu(gx)) @ mla_w_o[j]
            if not last:
                oc = block_attention(qc, kc, vc)
                yc = (oc * jax.nn.silu(gc)) @ mla_w_o[j]
        else:
            yx = conformer_conv_branch(hx, cv_w_in[j], cv_w_dw[j], cv_b_dw[j], cv_g_ln[j], cv_b_ln[j], cv_w_o[j])
            if not last:
                yc = conformer_conv_branch(hc, cv_w_in[j], cv_w_dw[j], cv_b_dw[j], cv_g_ln[j], cv_b_ln[j], cv_w_o[j])

        x = x + gate_x * rms_norm(yx, g_post[i])
        if not last:
            ctx = ctx + gate_c * rms_norm(yc, g_post[i])

    return x
```

```python
import functools
import math

import jax
import jax.numpy as jnp
from jax import lax
from jax.experimental import pallas as pl
from jax.experimental.pallas import tpu as pltpu

D_MODEL = 1024
DEPTH = 4
GRID_W = 64
N_MIXERS = 2
N_HEADS = 8
NOPE_DIM = 128
ROPE_DIM = 64
V_DIM = 128
Q_LORA = 384
KV_LORA = 256
ATTN_W = N_HEADS * V_DIM
ROPE_BASE = 10000.0
CONV_W = D_MODEL
CONV_K = 31
CONV_PAD = CONV_K // 2
EPS = 1e-6

LANES = 128
MOD_ROWS = 16
VMEM_LIMIT = 56 * 1024 * 1024

F32 = jnp.float32
BF16 = jnp.bfloat16

_CQ0, _CKV0, _KR0, _GATE0, _WIN_COLS = 0, Q_LORA, Q_LORA + KV_LORA, Q_LORA + KV_LORA + LANES, Q_LORA + KV_LORA + LANES + ATTN_W
_QN_COLS = N_HEADS * NOPE_DIM
_QR_COLS = N_HEADS * ROPE_DIM


def _silu(v):
    return v * jax.nn.sigmoid(v)


def _rms(v, g):
    return v * lax.rsqrt(jnp.mean(v * v, axis=-1, keepdims=True) + EPS) * g


def _dot(a, b):
    return jnp.dot(a, b, preferred_element_type=F32)


def _dot_nt(a, b):
    return lax.dot_general(a, b, (((1,), (1,)), ((), ())), preferred_element_type=F32)


def _rope(v, c, s1, s2):
    return v * c + pltpu.roll(v, 16, axis=1) * s1 + pltpu.roll(v, LANES - 16, axis=1) * s2


def _mod_row(ref, row):
    return ref[0, pl.ds(row, 1), :]


def _mod_kernel(c_ref, w_ref, b_ref, o_ref):
    sc = _silu(c_ref[...]).astype(BF16)
    o_ref[0] = _dot(sc, w_ref[0].astype(BF16)) + b_ref[0]


def _modulation(cvec, w_mod, b_mod):
    d = D_MODEL
    return pl.pallas_call(
        _mod_kernel,
        grid=(DEPTH, 3),
        in_specs=[
            pl.BlockSpec((MOD_ROWS, d), lambda i, j: (0, 0)),
            pl.BlockSpec((1, d, d), lambda i, j: (i, 0, j)),
            pl.BlockSpec((1, 1, d), lambda i, j: (i, 0, j)),
        ],
        out_specs=pl.BlockSpec((1, MOD_ROWS, d), lambda i, j: (i, 0, j)),
        out_shape=jax.ShapeDtypeStruct((DEPTH, MOD_ROWS, 3 * d), F32),
        compiler_params=pltpu.CompilerParams(vmem_limit_bytes=VMEM_LIMIT),
        name="modulation",
    )(cvec, w_mod, b_mod.reshape(DEPTH, 1, 3 * d))


def _mla_proj_kernel(ctx_row, x_ref, shift_ref, scale_ref, gpre_ref, win_ref, gq_ref, wuq_ref, gkv_ref,
                     wuk_ref, wuvt_ref, tc_ref, ts1_ref, ts2_ref,
                     qn_ref, qr_ref, kn_ref, kr_ref, vt_ref, sg_ref):
    row = pl.program_id(0) if ctx_row is None else ctx_row
    x = x_ref[0]
    h = _rms(x, gpre_ref[...]) * (1.0 + _mod_row(scale_ref, row)) + _mod_row(shift_ref, row)
    u = _dot(h.astype(BF16), win_ref[...])
    tc, ts1, ts2 = tc_ref[...], ts1_ref[...], ts2_ref[...]

    cqn = _rms(u[:, _CQ0:_CKV0], gq_ref[...]).astype(BF16)
    q = _dot(cqn, wuq_ref[...])
    qn_ref[0] = q[:, :_QN_COLS].astype(BF16)
    for j in range(_QR_COLS // LANES):
        lo = _QN_COLS + j * LANES
        qr_ref[0, :, j * LANES:(j + 1) * LANES] = _rope(q[:, lo:lo + LANES], tc, ts1, ts2).astype(BF16)

    ckvn = _rms(u[:, _CKV0:_KR0], gkv_ref[...]).astype(BF16)
    kn_ref[0] = _dot(ckvn, wuk_ref[...]).astype(BF16)
    vt_ref[0] = _dot_nt(wuvt_ref[...], ckvn).astype(BF16)
    kr_ref[0] = _rope(u[:, _KR0:_GATE0], tc, ts1, ts2).astype(BF16)
    sg_ref[0] = _silu(u[:, _GATE0:]).astype(BF16)


def _mla_proj(x, mod, layer, p, tables, ctx_row, tm):
    b, l, d = x.shape
    nt = l // tm
    const = lambda shape: pl.BlockSpec(shape, lambda i, t: (0,) * len(shape))
    modspec = lambda j: pl.BlockSpec((1, MOD_ROWS, d), lambda i, t: (layer, 0, j))
    tabspec = pl.BlockSpec((tm, LANES), lambda i, t: (t, 0))
    tok = lambda w: pl.BlockSpec((1, tm, w), lambda i, t: (i, t, 0))
    bf = lambda w: jax.ShapeDtypeStruct((b, l, w), BF16)
    return pl.pallas_call(
        functools.partial(_mla_proj_kernel, ctx_row),
        grid=(b, nt),
        in_specs=[tok(d), modspec(0), modspec(1), const((1, d)), const((d, _WIN_COLS)), const((1, Q_LORA)),
                  const((Q_LORA, _QN_COLS + _QR_COLS)), const((1, KV_LORA)), const((KV_LORA, _QN_COLS)),
                  const((ATTN_W, KV_LORA)), tabspec, tabspec, tabspec],
        out_specs=[tok(_QN_COLS), tok(_QR_COLS), tok(_QN_COLS), tok(LANES),
                   pl.BlockSpec((1, ATTN_W, tm), lambda i, t: (i, 0, t)), tok(ATTN_W)],
        out_shape=[bf(_QN_COLS), bf(_QR_COLS), bf(_QN_COLS), bf(LANES),
                   jax.ShapeDtypeStruct((b, ATTN_W, l), BF16), bf(ATTN_W)],
        compiler_params=pltpu.CompilerParams(vmem_limit_bytes=VMEM_LIMIT),
        name="mla_proj",
    )(x, mod, mod, p["g_pre"], p["w_in"], p["g_q"], p["w_uq"], p["g_kv"], p["w_uk"], p["w_uvt"], *tables)


def _attn_kernel(seg_lens, tq, qn_ref, qr_ref, sg_ref, *rest):
    nseg = len(seg_lens)
    seg_refs = rest[:3 * nseg]
    o_ref, k_scr, vt_scr = rest[3 * nseg:]
    head = pl.program_id(1)
    lane_half = lax.broadcasted_iota(jnp.int32, (1, LANES), 1) // ROPE_DIM
    keep = lane_half == (head % 2)

    off = 0
    for s, ls in enumerate(seg_lens):
        kn_ref, kr_ref, vt_ref = seg_refs[3 * s:3 * s + 3]
        k_scr[off:off + ls, :NOPE_DIM] = kn_ref[0]
        k_scr[off:off + ls, NOPE_DIM:] = jnp.where(keep, kr_ref[0], jnp.zeros((), BF16))
        vt_scr[:, off:off + ls] = vt_ref[0]
        off += ls

    def tile(t, carry):
        r = pl.multiple_of(t * tq, tq)
        q = jnp.concatenate([qn_ref[0, pl.ds(r, tq), :], qr_ref[0, pl.ds(r, tq), :]], axis=1)
        st = _dot_nt(k_scr[...], q)
        m = jnp.max(st, axis=0, keepdims=True)
        p = jnp.exp(st - m)
        l = jnp.sum(p, axis=0, keepdims=True)
        ot = _dot(vt_scr[...], p.astype(BF16)) / l
        o_ref[0, pl.ds(r, tq), :] = (ot.T * sg_ref[0, pl.ds(r, tq), :].astype(F32)).astype(BF16)
        return carry

    lax.fori_loop(0, qn_ref.shape[1] // tq, tile, 0)


def _attention(qn, qr, sg, segments, tq):
    b, lq, _ = qn.shape
    seg_lens = tuple(s[0].shape[1] for s in segments)
    lk = sum(seg_lens)
    in_specs = [
        pl.BlockSpec((1, lq, NOPE_DIM), lambda i, h: (i, 0, h)),
        pl.BlockSpec((1, lq, LANES), lambda i, h: (i, 0, h // 2)),
        pl.BlockSpec((1, lq, V_DIM), lambda i, h: (i, 0, h)),
    ]
    args = [qn, qr, sg]
    for kn, kr, vt in segments:
        ls = kn.shape[1]
        in_specs += [pl.BlockSpec((1, ls, NOPE_DIM), lambda i, h: (i, 0, h)),
                     pl.BlockSpec((1, ls, LANES), lambda i, h: (i, 0, 0)),
                     pl.BlockSpec((1, V_DIM, ls), lambda i, h: (i, h, 0))]
        args += [kn, kr, vt]
    return pl.pallas_call(
        functools.partial(_attn_kernel, seg_lens, tq),
        grid=(b, N_HEADS),
        in_specs=in_specs,
        out_specs=pl.BlockSpec((1, lq, V_DIM), lambda i, h: (i, 0, h)),
        out_shape=jax.ShapeDtypeStruct((b, lq, ATTN_W), BF16),
        scratch_shapes=[pltpu.VMEM((lk, NOPE_DIM + LANES), BF16), pltpu.VMEM((V_DIM, lk), BF16)],
        compiler_params=pltpu.CompilerParams(vmem_limit_bytes=VMEM_LIMIT),
        name="attention",
    )(*args)


def _out_kernel(ctx_row, og_ref, x_ref, gate_ref, wo_ref, gpost_ref, o_ref):
    row = pl.program_id(0) if ctx_row is None else ctx_row
    y = _dot(og_ref[0], wo_ref[...])
    o_ref[0] = x_ref[0] + _mod_row(gate_ref, row) * _rms(y, gpost_ref[...])


def _out_residual(og, x, mod, layer, w_o, g_post, ctx_row, tm):
    b, l, d = x.shape
    return pl.pallas_call(
        functools.partial(_out_kernel, ctx_row),
        grid=(b, l // tm),
        in_specs=[pl.BlockSpec((1, tm, og.shape[2]), lambda i, t: (i, t, 0)),
                  pl.BlockSpec((1, tm, d), lambda i, t: (i, t, 0)),
                  pl.BlockSpec((1, MOD_ROWS, d), lambda i, t: (layer, 0, 2)),
                  pl.BlockSpec(w_o.shape, lambda i, t: (0, 0)),
                  pl.BlockSpec((1, d), lambda i, t: (0, 0))],
        out_specs=pl.BlockSpec((1, tm, d), lambda i, t: (i, t, 0)),
        out_shape=jax.ShapeDtypeStruct((b, l, d), F32),
        compiler_params=pltpu.CompilerParams(vmem_limit_bytes=VMEM_LIMIT),
        name="out_residual",
    )(og, x, mod, w_o, g_post)


def _conv_in_kernel(ctx_row, x_ref, shift_ref, scale_ref, gpre_ref, win_ref, u_ref, sg_ref):
    row = pl.program_id(0) if ctx_row is None else ctx_row
    h = _rms(x_ref[0], gpre_ref[...]) * (1.0 + _mod_row(scale_ref, row)) + _mod_row(shift_ref, row)
    u3 = _dot(h.astype(BF16), win_ref[...])
    u_ref[0] = (u3[:, :CONV_W] * jax.nn.sigmoid(u3[:, CONV_W:2 * CONV_W])).astype(BF16)
    sg_ref[0] = _silu(u3[:, 2 * CONV_W:]).astype(BF16)


def _conv_in(x, mod, layer, g_pre, w_in, ctx_row, tm):
    b, l, d = x.shape
    tok = lambda w: pl.BlockSpec((1, tm, w), lambda i, t: (i, t, 0))
    modspec = lambda j: pl.BlockSpec((1, MOD_ROWS, d), lambda i, t: (layer, 0, j))
    return pl.pallas_call(
        functools.partial(_conv_in_kernel, ctx_row),
        grid=(b, l // tm),
        in_specs=[tok(d), modspec(0), modspec(1), pl.BlockSpec((1, d), lambda i, t: (0, 0)),
                  pl.BlockSpec(w_in.shape, lambda i, t: (0, 0))],
        out_specs=[tok(CONV_W), tok(CONV_W)],
        out_shape=[jax.ShapeDtypeStruct((b, l, CONV_W), BF16)] * 2,
        compiler_params=pltpu.CompilerParams(vmem_limit_bytes=VMEM_LIMIT),
        name="conv_in",
    )(x, mod, mod, g_pre, w_in)


_HALO = 16
_CONV_ROWS = 32
_CONV_LANES = 256
_SUBLANES = 8


def _conv_out_kernel(ctx_row, tm, u_ref, sg_ref, x_ref, gate_ref, wdw_ref, bdw_ref, gln_ref, bln_ref,
                     wo_ref, gpost_ref, o_ref, win, cv):
    row = pl.program_id(0) if ctx_row is None else ctx_row
    t = pl.program_id(1)
    nt = pl.num_programs(1)
    l = u_ref.shape[1]
    r0 = pl.multiple_of(t * tm, tm)

    top0 = pl.multiple_of(jnp.maximum(r0 - _HALO, 0), _HALO)
    bot0 = pl.multiple_of(jnp.minimum(r0 + tm, l - _HALO), _HALO)
    win[0:_HALO] = u_ref[0, pl.ds(top0, _HALO), :].astype(F32) * (t > 0).astype(F32)
    win[_HALO:_HALO + tm] = u_ref[0, pl.ds(r0, tm), :].astype(F32)
    win[_HALO + tm:] = u_ref[0, pl.ds(bot0, _HALO), :].astype(F32) * (t < nt - 1).astype(F32)

    span = _CONV_ROWS + (CONV_K // _SUBLANES) * _SUBLANES
    for rc in range(tm // _CONV_ROWS):
        for lc in range(CONV_W // _CONV_LANES):
            ls = slice(lc * _CONV_LANES, (lc + 1) * _CONV_LANES)
            acc = jnp.broadcast_to(bdw_ref[:, ls], (_CONV_ROWS, _CONV_LANES))
            for r in range(_SUBLANES):
                wr = win[rc * _CONV_ROWS + r:rc * _CONV_ROWS + r + span, ls]
                for a in range(CONV_K // _SUBLANES + 1):
                    k = a * _SUBLANES + r - 1
                    if 0 <= k < CONV_K:
                        acc = acc + wr[a * _SUBLANES:a * _SUBLANES + _CONV_ROWS] * wdw_ref[k:k + 1, ls]
            cv[rc * _CONV_ROWS:(rc + 1) * _CONV_ROWS, ls] = acc

    c = cv[...]
    mu = jnp.mean(c, axis=-1, keepdims=True)
    cc = c - mu
    var = jnp.mean(cc * cc, axis=-1, keepdims=True)
    ln = cc * lax.rsqrt(var + EPS) * gln_ref[...] + bln_ref[...]
    z = (_silu(ln) * sg_ref[0].astype(F32)).astype(BF16)
    y = _dot(z, wo_ref[...])
    o_ref[0] = x_ref[0] + _mod_row(gate_ref, row) * _rms(y, gpost_ref[...])


def _conv_out(u, sg, x, mod, layer, p, ctx_row, tm):
    b, l, d = x.shape
    tok = lambda w: pl.BlockSpec((1, tm, w), lambda i, t: (i, t, 0))
    vec = pl.BlockSpec((1, CONV_W), lambda i, t: (0, 0))
    return pl.pallas_call(
        functools.partial(_conv_out_kernel, ctx_row, tm),
        grid=(b, l // tm),
        in_specs=[pl.BlockSpec((1, l, CONV_W), lambda i, t: (i, 0, 0)), tok(CONV_W), tok(d),
                  pl.BlockSpec((1, MOD_ROWS, d), lambda i, t: (layer, 0, 2)),
                  pl.BlockSpec((CONV_K, CONV_W), lambda i, t: (0, 0)), vec, vec, vec,
                  pl.BlockSpec((CONV_W, d), lambda i, t: (0, 0)), pl.BlockSpec((1, d), lambda i, t: (0, 0))],
        out_specs=tok(d),
        out_shape=jax.ShapeDtypeStruct((b, l, d), F32),
        scratch_shapes=[pltpu.VMEM((tm + 2 * _HALO, CONV_W), F32), pltpu.VMEM((tm, CONV_W), F32)],
        compiler_params=pltpu.CompilerParams(vmem_limit_bytes=VMEM_LIMIT),
        name="conv_out",
    )(u, sg, x, mod, p["w_dw"], p["b_dw"], p["g_ln"], p["b_ln"], p["w_o"], p["g_post"])


def _rope_tables(n_tokens):
    rows_n = n_tokens // GRID_W
    rows = jnp.repeat(jnp.arange(rows_n, dtype=F32), GRID_W)
    cols = jnp.tile(jnp.arange(GRID_W, dtype=F32), rows_n)
    half = ROPE_DIM // 2
    freqs = 1.0 / (ROPE_BASE ** (jnp.arange(0, half, 2, dtype=F32) / half))
    ang_r = rows[:, None] * freqs[None, :]
    ang_c = cols[:, None] * freqs[None, :]
    zero = jnp.zeros_like(ang_r)
    cos64 = jnp.concatenate([jnp.cos(ang_r), jnp.cos(ang_r), jnp.cos(ang_c), jnp.cos(ang_c)], axis=-1)
    s1_64 = jnp.concatenate([zero, jnp.sin(ang_r), zero, jnp.sin(ang_c)], axis=-1)
    s2_64 = jnp.concatenate([-jnp.sin(ang_r), zero, -jnp.sin(ang_c), zero], axis=-1)
    dup = lambda a: jnp.concatenate([a, a], axis=-1)
    return dup(cos64), dup(s1_64), dup(s2_64)


def _identity_tables(n_tokens):
    one = jnp.ones((n_tokens, LANES), F32)
    zero = jnp.zeros((n_tokens, LANES), F32)
    return one, zero, zero


def _mla_params(j, g_pre, mla_w_in, mla_g_q, mla_w_uq, mla_g_kv, mla_w_ukv, mla_w_o):
    w_in = mla_w_in[j]
    kr = w_in[:, _KR0:_KR0 + ROPE_DIM]
    w_in_ext = jnp.concatenate([w_in[:, :_KR0], kr, kr, w_in[:, _KR0 + ROPE_DIM:]], axis=1).astype(BF16)
    w_uq = mla_w_uq[j].reshape(Q_LORA, N_HEADS, NOPE_DIM + ROPE_DIM)
    w_uq = jnp.concatenate([w_uq[:, :, :NOPE_DIM].reshape(Q_LORA, _QN_COLS),
                            w_uq[:, :, NOPE_DIM:].reshape(Q_LORA, _QR_COLS)], axis=1).astype(BF16)
    w_ukv = mla_w_ukv[j].reshape(KV_LORA, N_HEADS, NOPE_DIM + V_DIM)
    w_uk = w_ukv[:, :, :NOPE_DIM].reshape(KV_LORA, _QN_COLS).astype(BF16)
    w_uvt = w_ukv[:, :, NOPE_DIM:].reshape(KV_LORA, ATTN_W).T.astype(BF16)
    scale = 1.0 / math.sqrt(NOPE_DIM + ROPE_DIM)
    return dict(g_pre=g_pre.reshape(1, -1), w_in=w_in_ext, g_q=(mla_g_q[j] * scale).reshape(1, -1), w_uq=w_uq,
                g_kv=mla_g_kv[j].reshape(1, -1), w_uk=w_uk, w_uvt=w_uvt, w_o=mla_w_o[j].astype(BF16))


def kernel(x, c, ctx, c_ctx, w_mod, b_mod, g_pre, g_post, mla_w_in, mla_g_q, mla_w_uq, mla_g_kv, mla_w_ukv,
           mla_w_o, cv_w_in, cv_w_dw, cv_b_dw, cv_g_ln, cv_b_ln, cv_w_o):
    b, s, d = x.shape
    lc = ctx.shape[1]
    ctx_row = b
    assert b < MOD_ROWS and s % GRID_W == 0

    cvec = jnp.zeros((MOD_ROWS, d), F32).at[:b].set(c).at[ctx_row].set(c_ctx)
    mod = _modulation(cvec, w_mod, b_mod)

    tables_x = _rope_tables(s)
    tables_c = _identity_tables(lc)

    for i in range(DEPTH):
        last = i == DEPTH - 1
        j = i // N_MIXERS
        gp = g_post[i].reshape(1, -1)
        if i % N_MIXERS == 0:
            p = _mla_params(j, g_pre[i], mla_w_in, mla_g_q, mla_w_uq, mla_g_kv, mla_w_ukv, mla_w_o)
            qn, qr, kn, kr, vt, sg = _mla_proj(x, mod, i, p, tables_x, None, 512)
            qn_c, qr_c, kn_c, kr_c, vt_c, sg_c = _mla_proj(ctx, mod, i, p, tables_c, ctx_row, lc)
            og = _attention(qn, qr, sg, [(kn, kr, vt), (kn_c, kr_c, vt_c)], 256)
            x = _out_residual(og, x, mod, i, p["w_o"], gp, None, 512)
            if not last:
                og_c = _attention(qn_c, qr_c, sg_c, [(kn_c, kr_c, vt_c)], lc)
                ctx = _out_residual(og_c, ctx, mod, i, p["w_o"], gp, ctx_row, lc)
        else:
            p = dict(w_dw=cv_w_dw[j].reshape(CONV_K, CONV_W), b_dw=cv_b_dw[j].reshape(1, -1),
                     g_ln=cv_g_ln[j].reshape(1, -1), b_ln=cv_b_ln[j].reshape(1, -1),
                     w_o=cv_w_o[j].astype(BF16), g_post=gp)
            w_in = cv_w_in[j].astype(BF16)
            gpre = g_pre[i].reshape(1, -1)
            u, sg = _conv_in(x, mod, i, gpre, w_in, None, 512)
            x = _conv_out(u, sg, x, mod, i, p, None, 256)
            if not last:
                u_c, sg_c = _conv_in(ctx, mod, i, gpre, w_in, ctx_row, lc)
                ctx = _conv_out(u_c, sg_c, ctx, mod, i, p, ctx_row, lc)
    return x
```

```python
import functools
import math

import jax
import jax.numpy as jnp
from jax import lax
from jax.experimental import pallas as pl
from jax.experimental.pallas import tpu as pltpu

D_MODEL = 1024
DEPTH = 4
GRID_W = 64
N_MIXERS = 2
N_HEADS = 8
NOPE_DIM = 128
ROPE_DIM = 64
V_DIM = 128
Q_LORA = 384
KV_LORA = 256
ATTN_W = N_HEADS * V_DIM
ROPE_BASE = 10000.0
CONV_W = D_MODEL
CONV_K = 31
CONV_PAD = CONV_K // 2
EPS = 1e-6

LANES = 128
MOD_ROWS = 16
VMEM_LIMIT = 56 * 1024 * 1024
_HEAD_PAIR = 2
_ONES_ROWS = 16
_PROB_ROWS = 64

F32 = jnp.float32
BF16 = jnp.bfloat16

_CQ0, _CKV0, _KR0, _GATE0, _WIN_COLS = 0, Q_LORA, Q_LORA + KV_LORA, Q_LORA + KV_LORA + LANES, Q_LORA + KV_LORA + LANES + ATTN_W
_QN_COLS = N_HEADS * NOPE_DIM
_QR_COLS = N_HEADS * ROPE_DIM


def _silu(v):
    return v * jax.nn.sigmoid(v)


def _rms(v, g):
    return v * lax.rsqrt(jnp.mean(v * v, axis=-1, keepdims=True) + EPS) * g


def _dot(a, b):
    return jnp.dot(a, b, preferred_element_type=F32)


def _dot_nt(a, b):
    return lax.dot_general(a, b, (((1,), (1,)), ((), ())), preferred_element_type=F32)


def _rope(v, c, s1, s2):
    return v * c + pltpu.roll(v, 16, axis=1) * s1 + pltpu.roll(v, LANES - 16, axis=1) * s2


def _mod_row(ref, row):
    return ref[0, pl.ds(row, 1), :]


def _mod_kernel(c_ref, w_ref, b_ref, o_ref):
    sc = _silu(c_ref[...]).astype(BF16)
    o_ref[0] = _dot(sc, w_ref[0].astype(BF16)) + b_ref[0]


def _modulation(cvec, w_mod, b_mod):
    d = D_MODEL
    return pl.pallas_call(
        _mod_kernel,
        grid=(DEPTH, 3),
        in_specs=[
            pl.BlockSpec((MOD_ROWS, d), lambda i, j: (0, 0)),
            pl.BlockSpec((1, d, d), lambda i, j: (i, 0, j)),
            pl.BlockSpec((1, 1, d), lambda i, j: (i, 0, j)),
        ],
        out_specs=pl.BlockSpec((1, MOD_ROWS, d), lambda i, j: (i, 0, j)),
        out_shape=jax.ShapeDtypeStruct((DEPTH, MOD_ROWS, 3 * d), F32),
        compiler_params=pltpu.CompilerParams(vmem_limit_bytes=VMEM_LIMIT),
        name="modulation",
    )(cvec, w_mod, b_mod.reshape(DEPTH, 1, 3 * d))


def _mla_proj_kernel(ctx_row, x_ref, shift_ref, scale_ref, gpre_ref, win_ref, gq_ref, wuq_ref, gkv_ref,
                     wuk_ref, wuvt_ref, tc_ref, ts1_ref, ts2_ref,
                     qn_ref, qr_ref, kn_ref, kr_ref, vt_ref, sg_ref):
    row = pl.program_id(0) if ctx_row is None else ctx_row
    x = x_ref[0]
    h = _rms(x, gpre_ref[...]) * (1.0 + _mod_row(scale_ref, row)) + _mod_row(shift_ref, row)
    u = _dot(h.astype(BF16), win_ref[...])
    tc, ts1, ts2 = tc_ref[...], ts1_ref[...], ts2_ref[...]

    cqn = _rms(u[:, _CQ0:_CKV0], gq_ref[...]).astype(BF16)
    q = _dot(cqn, wuq_ref[...])
    for j in range(_QR_COLS // LANES):
        lo = _QN_COLS + j * LANES
        qr_ref[0, j] = _rope(q[:, lo:lo + LANES], tc, ts1, ts2).astype(BF16)

    ckvn = _rms(u[:, _CKV0:_KR0], gkv_ref[...]).astype(BF16)
    kn = _dot(ckvn, wuk_ref[...])
    sg = _silu(u[:, _GATE0:])
    for hd in range(N_HEADS):
        hs = slice(hd * NOPE_DIM, (hd + 1) * NOPE_DIM)
        qn_ref[0, hd] = q[:, hs].astype(BF16)
        kn_ref[0, hd] = kn[:, hs].astype(BF16)
        sg_ref[0, hd] = sg[:, hs].astype(BF16)
    vt_ref[0] = _dot_nt(wuvt_ref[...], ckvn).astype(BF16)
    kr_ref[0] = _rope(u[:, _KR0:_GATE0], tc, ts1, ts2).astype(BF16)


def _mla_proj(x, mod, layer, p, tables, ctx_row, tm):
    b, l, d = x.shape
    nt = l // tm
    const = lambda shape: pl.BlockSpec(shape, lambda i, t: (0,) * len(shape))
    modspec = lambda j: pl.BlockSpec((1, MOD_ROWS, d), lambda i, t: (layer, 0, j))
    tabspec = pl.BlockSpec((tm, LANES), lambda i, t: (t, 0))
    tok = lambda w: pl.BlockSpec((1, tm, w), lambda i, t: (i, t, 0))
    heads = lambda n: pl.BlockSpec((1, n, tm, LANES), lambda i, t: (i, 0, t, 0))
    hshape = lambda n: jax.ShapeDtypeStruct((b, n, l, LANES), BF16)
    nh, npair = N_HEADS, N_HEADS // _HEAD_PAIR
    return pl.pallas_call(
        functools.partial(_mla_proj_kernel, ctx_row),
        grid=(b, nt),
        in_specs=[tok(d), modspec(0), modspec(1), const((1, d)), const((d, _WIN_COLS)), const((1, Q_LORA)),
                  const((Q_LORA, _QN_COLS + _QR_COLS)), const((1, KV_LORA)), const((KV_LORA, _QN_COLS)),
                  const((ATTN_W, KV_LORA)), tabspec, tabspec, tabspec],
        out_specs=[heads(nh), heads(npair), heads(nh), tok(LANES),
                   pl.BlockSpec((1, ATTN_W, tm), lambda i, t: (i, 0, t)), heads(nh)],
        out_shape=[hshape(nh), hshape(npair), hshape(nh), jax.ShapeDtypeStruct((b, l, LANES), BF16),
                   jax.ShapeDtypeStruct((b, ATTN_W, l), BF16), hshape(nh)],
        compiler_params=pltpu.CompilerParams(vmem_limit_bytes=VMEM_LIMIT),
        name="mla_proj",
    )(x, mod, mod, p["g_pre"], p["w_in"], p["g_q"], p["w_uq"], p["g_kv"], p["w_uk"], p["w_uvt"], *tables)


def _attn_kernel(seg_lens, tq, qn_ref, qr_ref, sg_ref, *rest):
    nseg = len(seg_lens)
    seg_refs = rest[:3 * nseg]
    o_ref, k_scr, vt_scr = rest[3 * nseg:3 * nseg + 3]
    st_scr, m_scr, p_scr = (rest[3 * nseg + 3 + 2 * i:3 * nseg + 5 + 2 * i] for i in range(3))
    lane_half = lax.broadcasted_iota(jnp.int32, (1, LANES), 1) // ROPE_DIM
    nt = qn_ref.shape[2] // tq
    n_units = _HEAD_PAIR * nt
    assert n_units % 2 == 0

    for hh in range(_HEAD_PAIR):
        off = 0
        for s, ls in enumerate(seg_lens):
            kn_ref, kr_ref, vt_ref = seg_refs[3 * s:3 * s + 3]
            k_scr[hh, off:off + ls, :NOPE_DIM] = kn_ref[0, hh]
            k_scr[hh, off:off + ls, NOPE_DIM:] = jnp.where(lane_half == hh, kr_ref[0], jnp.zeros((), BF16))
            vt_scr[hh, :V_DIM, off:off + ls] = vt_ref[0, hh * V_DIM:(hh + 1) * V_DIM, :]
            off += ls
        vt_scr[hh, V_DIM:, :] = jnp.ones((_ONES_ROWS, off), BF16)

    def unit(u):
        return u // nt, pl.ds(pl.multiple_of((u % nt) * tq, tq), tq)

    def scores(u, slot):
        hh, rows = unit(u)
        q = jnp.concatenate([qn_ref[0, hh, rows, :], qr_ref[0, 0, rows, :]], axis=1)
        st = _dot_nt(k_scr[hh], q)
        st_scr[slot][...] = st
        m_scr[slot][...] = jnp.max(st, axis=0, keepdims=True)

    def probs(slot):
        m = m_scr[slot][...]
        for c in range(0, st_scr[slot].shape[0], _PROB_ROWS):
            rows = slice(c, c + _PROB_ROWS)
            p_scr[slot][rows, :] = jnp.exp2((st_scr[slot][rows, :] - m).astype(BF16))

    def values(u, slot):
        hh, rows = unit(u)
        ot = _dot(vt_scr[hh], p_scr[slot][...])
        ot = ot[:V_DIM] / ot[V_DIM:V_DIM + 1]
        o_ref[0, hh, rows, :] = (ot.T * sg_ref[0, hh, rows, :].astype(F32)).astype(BF16)

    scores(0, 0)
    scores(1, 1)
    probs(0)

    def two_iterations(j, carry):
        for slot in range(2):
            scores(2 * j + slot, slot)
            probs(1 - slot)
            values(2 * j + slot - 2, slot)
        return carry

    lax.fori_loop(1, n_units // 2, two_iterations, 0)
    probs(1)
    values(n_units - 2, 0)
    values(n_units - 1, 1)


def _attention(qn, qr, sg, segments, tq):
    b, nh, lq, _ = qn.shape
    seg_lens = tuple(s[0].shape[2] for s in segments)
    lk = sum(seg_lens)
    hp = _HEAD_PAIR
    pair = lambda l: pl.BlockSpec((1, hp, l, LANES), lambda i, h: (i, h, 0, 0))
    in_specs = [pair(lq), pl.BlockSpec((1, 1, lq, LANES), lambda i, h: (i, h, 0, 0)), pair(lq)]
    args = [qn, qr, sg]
    for kn, kr, vt in segments:
        ls = kn.shape[2]
        in_specs += [pair(ls), pl.BlockSpec((1, ls, LANES), lambda i, h: (i, 0, 0)),
                     pl.BlockSpec((1, hp * V_DIM, ls), lambda i, h: (i, h, 0))]
        args += [kn, kr, vt]
    return pl.pallas_call(
        functools.partial(_attn_kernel, seg_lens, tq),
        grid=(b, nh // hp),
        in_specs=in_specs,
        out_specs=pair(lq),
        out_shape=jax.ShapeDtypeStruct((b, nh, lq, LANES), BF16),
        scratch_shapes=[pltpu.VMEM((hp, lk, NOPE_DIM + LANES), BF16),
                        pltpu.VMEM((hp, V_DIM + _ONES_ROWS, lk), BF16),
                        pltpu.VMEM((lk, tq), F32), pltpu.VMEM((lk, tq), F32),
                        pltpu.VMEM((1, tq), F32), pltpu.VMEM((1, tq), F32),
                        pltpu.VMEM((lk, tq), BF16), pltpu.VMEM((lk, tq), BF16)],
        compiler_params=pltpu.CompilerParams(vmem_limit_bytes=VMEM_LIMIT),
        name="attention",
    )(*args)


def _out_kernel(ctx_row, og_ref, x_ref, gate_ref, wo_ref, gpost_ref, o_ref):
    row = pl.program_id(0) if ctx_row is None else ctx_row
    og = jnp.concatenate([og_ref[0, hd] for hd in range(og_ref.shape[1])], axis=1)
    y = _dot(og, wo_ref[...])
    o_ref[0] = x_ref[0] + _mod_row(gate_ref, row) * _rms(y, gpost_ref[...])


def _out_residual(og, x, mod, layer, w_o, g_post, ctx_row, tm):
    b, l, d = x.shape
    return pl.pallas_call(
        functools.partial(_out_kernel, ctx_row),
        grid=(b, l // tm),
        in_specs=[pl.BlockSpec((1, og.shape[1], tm, LANES), lambda i, t: (i, 0, t, 0)),
                  pl.BlockSpec((1, tm, d), lambda i, t: (i, t, 0)),
                  pl.BlockSpec((1, MOD_ROWS, d), lambda i, t: (layer, 0, 2)),
                  pl.BlockSpec(w_o.shape, lambda i, t: (0, 0)),
                  pl.BlockSpec((1, d), lambda i, t: (0, 0))],
        out_specs=pl.BlockSpec((1, tm, d), lambda i, t: (i, t, 0)),
        out_shape=jax.ShapeDtypeStruct((b, l, d), F32),
        compiler_params=pltpu.CompilerParams(vmem_limit_bytes=VMEM_LIMIT),
        name="out_residual",
    )(og, x, mod, w_o, g_post)


def _conv_in_kernel(ctx_row, x_ref, shift_ref, scale_ref, gpre_ref, win_ref, u_ref, sg_ref):
    row = pl.program_id(0) if ctx_row is None else ctx_row
    h = _rms(x_ref[0], gpre_ref[...]) * (1.0 + _mod_row(scale_ref, row)) + _mod_row(shift_ref, row)
    u3 = _dot(h.astype(BF16), win_ref[...])
    u_ref[0] = (u3[:, :CONV_W] * jax.nn.sigmoid(u3[:, CONV_W:2 * CONV_W])).astype(BF16)
    sg_ref[0] = _silu(u3[:, 2 * CONV_W:]).astype(BF16)


def _conv_in(x, mod, layer, g_pre, w_in, ctx_row, tm):
    b, l, d = x.shape
    tok = lambda w: pl.BlockSpec((1, tm, w), lambda i, t: (i, t, 0))
    modspec = lambda j: pl.BlockSpec((1, MOD_ROWS, d), lambda i, t: (layer, 0, j))
    return pl.pallas_call(
        functools.partial(_conv_in_kernel, ctx_row),
        grid=(b, l // tm),
        in_specs=[tok(d), modspec(0), modspec(1), pl.BlockSpec((1, d), lambda i, t: (0, 0)),
                  pl.BlockSpec(w_in.shape, lambda i, t: (0, 0))],
        out_specs=[tok(CONV_W), tok(CONV_W)],
        out_shape=[jax.ShapeDtypeStruct((b, l, CONV_W), BF16)] * 2,
        compiler_params=pltpu.CompilerParams(vmem_limit_bytes=VMEM_LIMIT),
        name="conv_in",
    )(x, mod, mod, g_pre, w_in)


_HALO = 16
_CONV_ROWS = 32
_CONV_LANES = 256
_SUBLANES = 8


def _conv_out_kernel(ctx_row, tm, u_ref, sg_ref, x_ref, gate_ref, wdw_ref, bdw_ref, gln_ref, bln_ref,
                     wo_ref, gpost_ref, o_ref, win, cv):
    row = pl.program_id(0) if ctx_row is None else ctx_row
    t = pl.program_id(1)
    nt = pl.num_programs(1)
    l = u_ref.shape[1]
    r0 = pl.multiple_of(t * tm, tm)

    top0 = pl.multiple_of(jnp.maximum(r0 - _HALO, 0), _HALO)
    bot0 = pl.multiple_of(jnp.minimum(r0 + tm, l - _HALO), _HALO)
    win[0:_HALO] = u_ref[0, pl.ds(top0, _HALO), :].astype(F32) * (t > 0).astype(F32)
    win[_HALO:_HALO + tm] = u_ref[0, pl.ds(r0, tm), :].astype(F32)
    win[_HALO + tm:] = u_ref[0, pl.ds(bot0, _HALO), :].astype(F32) * (t < nt - 1).astype(F32)

    span = _CONV_ROWS + (CONV_K // _SUBLANES + 1) * _SUBLANES
    for rc in range(tm // _CONV_ROWS):
        for lc in range(CONV_W // _CONV_LANES):
            ls = slice(lc * _CONV_LANES, (lc + 1) * _CONV_LANES)
            acc = jnp.broadcast_to(bdw_ref[:, ls], (_CONV_ROWS, _CONV_LANES))
            base = win[rc * _CONV_ROWS:rc * _CONV_ROWS + span, ls]
            for r in range(_SUBLANES):
                wr = base if r == 0 else pltpu.roll(base, span - r, axis=0)
                for a in range(CONV_K // _SUBLANES + 1):
                    k = a * _SUBLANES + r - 1
                    if 0 <= k < CONV_K:
                        acc = acc + wr[a * _SUBLANES:a * _SUBLANES + _CONV_ROWS] * wdw_ref[k:k + 1, ls]
            cv[rc * _CONV_ROWS:(rc + 1) * _CONV_ROWS, ls] = acc

    c = cv[...]
    mu = jnp.mean(c, axis=-1, keepdims=True)
    cc = c - mu
    var = jnp.mean(cc * cc, axis=-1, keepdims=True)
    ln = cc * lax.rsqrt(var + EPS) * gln_ref[...] + bln_ref[...]
    z = (_silu(ln) * sg_ref[0].astype(F32)).astype(BF16)
    y = _dot(z, wo_ref[...])
    o_ref[0] = x_ref[0] + _mod_row(gate_ref, row) * _rms(y, gpost_ref[...])


def _conv_out(u, sg, x, mod, layer, p, ctx_row, tm):
    b, l, d = x.shape
    tok = lambda w: pl.BlockSpec((1, tm, w), lambda i, t: (i, t, 0))
    vec = pl.BlockSpec((1, CONV_W), lambda i, t: (0, 0))
    return pl.pallas_call(
        functools.partial(_conv_out_kernel, ctx_row, tm),
        grid=(b, l // tm),
        in_specs=[pl.BlockSpec((1, l, CONV_W), lambda i, t: (i, 0, 0)), tok(CONV_W), tok(d),
                  pl.BlockSpec((1, MOD_ROWS, d), lambda i, t: (layer, 0, 2)),
                  pl.BlockSpec((CONV_K, CONV_W), lambda i, t: (0, 0)), vec, vec, vec,
                  pl.BlockSpec((CONV_W, d), lambda i, t: (0, 0)), pl.BlockSpec((1, d), lambda i, t: (0, 0))],
        out_specs=tok(d),
        out_shape=jax.ShapeDtypeStruct((b, l, d), F32),
        scratch_shapes=[pltpu.VMEM((tm + 2 * _HALO, CONV_W), F32), pltpu.VMEM((tm, CONV_W), F32)],
        compiler_params=pltpu.CompilerParams(vmem_limit_bytes=VMEM_LIMIT),
        name="conv_out",
    )(u, sg, x, mod, p["w_dw"], p["b_dw"], p["g_ln"], p["b_ln"], p["w_o"], p["g_post"])


def _rope_tables(n_tokens):
    rows_n = n_tokens // GRID_W
    rows = jnp.repeat(jnp.arange(rows_n, dtype=F32), GRID_W)
    cols = jnp.tile(jnp.arange(GRID_W, dtype=F32), rows_n)
    half = ROPE_DIM // 2
    freqs = 1.0 / (ROPE_BASE ** (jnp.arange(0, half, 2, dtype=F32) / half))
    ang_r = rows[:, None] * freqs[None, :]
    ang_c = cols[:, None] * freqs[None, :]
    zero = jnp.zeros_like(ang_r)
    cos64 = jnp.concatenate([jnp.cos(ang_r), jnp.cos(ang_r), jnp.cos(ang_c), jnp.cos(ang_c)], axis=-1)
    s1_64 = jnp.concatenate([zero, jnp.sin(ang_r), zero, jnp.sin(ang_c)], axis=-1)
    s2_64 = jnp.concatenate([-jnp.sin(ang_r), zero, -jnp.sin(ang_c), zero], axis=-1)
    dup = lambda a: jnp.concatenate([a, a], axis=-1)
    return dup(cos64), dup(s1_64), dup(s2_64)


def _identity_tables(n_tokens):
    one = jnp.ones((n_tokens, LANES), F32)
    zero = jnp.zeros((n_tokens, LANES), F32)
    return one, zero, zero


def _mla_params(j, g_pre, mla_w_in, mla_g_q, mla_w_uq, mla_g_kv, mla_w_ukv, mla_w_o):
    w_in = mla_w_in[j]
    kr = w_in[:, _KR0:_KR0 + ROPE_DIM]
    w_in_ext = jnp.concatenate([w_in[:, :_KR0], kr, kr, w_in[:, _KR0 + ROPE_DIM:]], axis=1).astype(BF16)
    w_uq = mla_w_uq[j].reshape(Q_LORA, N_HEADS, NOPE_DIM + ROPE_DIM)
    w_uq = jnp.concatenate([w_uq[:, :, :NOPE_DIM].reshape(Q_LORA, _QN_COLS),
                            w_uq[:, :, NOPE_DIM:].reshape(Q_LORA, _QR_COLS)], axis=1).astype(BF16)
    w_ukv = mla_w_ukv[j].reshape(KV_LORA, N_HEADS, NOPE_DIM + V_DIM)
    w_uk = w_ukv[:, :, :NOPE_DIM].reshape(KV_LORA, _QN_COLS).astype(BF16)
    w_uvt = w_ukv[:, :, NOPE_DIM:].reshape(KV_LORA, ATTN_W).T.astype(BF16)
    scale = math.log2(math.e) / math.sqrt(NOPE_DIM + ROPE_DIM)
    return dict(g_pre=g_pre.reshape(1, -1), w_in=w_in_ext, g_q=(mla_g_q[j] * scale).reshape(1, -1), w_uq=w_uq,
                g_kv=mla_g_kv[j].reshape(1, -1), w_uk=w_uk, w_uvt=w_uvt, w_o=mla_w_o[j].astype(BF16))


def kernel(x, c, ctx, c_ctx, w_mod, b_mod, g_pre, g_post, mla_w_in, mla_g_q, mla_w_uq, mla_g_kv, mla_w_ukv,
           mla_w_o, cv_w_in, cv_w_dw, cv_b_dw, cv_g_ln, cv_b_ln, cv_w_o):
    b, s, d = x.shape
    lc = ctx.shape[1]
    ctx_row = b
    assert b < MOD_ROWS and s % GRID_W == 0

    cvec = jnp.zeros((MOD_ROWS, d), F32).at[:b].set(c).at[ctx_row].set(c_ctx)
    mod = _modulation(cvec, w_mod, b_mod)

    tables_x = _rope_tables(s)
    tables_c = _identity_tables(lc)

    for i in range(DEPTH):
        last = i == DEPTH - 1
        j = i // N_MIXERS
        gp = g_post[i].reshape(1, -1)
        if i % N_MIXERS == 0:
            p = _mla_params(j, g_pre[i], mla_w_in, mla_g_q, mla_w_uq, mla_g_kv, mla_w_ukv, mla_w_o)
            qn, qr, kn, kr, vt, sg = _mla_proj(x, mod, i, p, tables_x, None, 512)
            qn_c, qr_c, kn_c, kr_c, vt_c, sg_c = _mla_proj(ctx, mod, i, p, tables_c, ctx_row, lc)
            og = _attention(qn, qr, sg, [(kn, kr, vt), (kn_c, kr_c, vt_c)], 512)
            x = _out_residual(og, x, mod, i, p["w_o"], gp, None, 512)
            if not last:
                og_c = _attention(qn_c, qr_c, sg_c, [(kn_c, kr_c, vt_c)], lc)
                ctx = _out_residual(og_c, ctx, mod, i, p["w_o"], gp, ctx_row, lc)
        else:
            p = dict(w_dw=cv_w_dw[j].reshape(CONV_K, CONV_W), b_dw=cv_b_dw[j].reshape(1, -1),
                     g_ln=cv_g_ln[j].reshape(1, -1), b_ln=cv_b_ln[j].reshape(1, -1),
                     w_o=cv_w_o[j].astype(BF16), g_post=gp)
            w_in = cv_w_in[j].astype(BF16)
            gpre = g_pre[i].reshape(1, -1)
            u, sg = _conv_in(x, mod, i, gpre, w_in, None, 512)
            x = _conv_out(u, sg, x, mod, i, p, None, 256)
            if not last:
                u_c, sg_c = _conv_in(ctx, mod, i, gpre, w_in, ctx_row, lc)
                ctx = _conv_out(u_c, sg_c, ctx, mod, i, p, ctx_row, lc)
    return x
```

```python
import functools
import math

import jax
import jax.numpy as jnp
import numpy as np
from jax import lax
from jax.experimental import pallas as pl
from jax.experimental.pallas import tpu as pltpu

D_MODEL = 1024
DEPTH = 4
GRID_W = 64
N_MIXERS = 2
N_HEADS = 8
NOPE_DIM = 128
ROPE_DIM = 64
V_DIM = 128
Q_LORA = 384
KV_LORA = 256
ATTN_W = N_HEADS * V_DIM
ROPE_BASE = 10000.0
CONV_W = D_MODEL
CONV_K = 31
CONV_PAD = CONV_K // 2
EPS = 1e-6

LANES = 128
MOD_ROWS = 16
VMEM_LIMIT = 56 * 1024 * 1024
_HEAD_PAIR = 2
_ONES_ROWS = 16
_PROB_ROWS = 64
_TM_PROJ = 512
_TM_CONV = 256
_TQ = 512
_HALO = 16
_CONV_ROWS = 64
assert _HALO >= CONV_PAD + 1

F32 = jnp.float32
BF16 = jnp.bfloat16

_CQ0, _CKV0, _KR0, _GATE0 = 0, Q_LORA, Q_LORA + KV_LORA, Q_LORA + KV_LORA + LANES
_WIN_COLS = _GATE0 + ATTN_W
_QN_COLS = N_HEADS * NOPE_DIM
_QR_COLS = N_HEADS * ROPE_DIM


def _silu(v):
    return v * jax.nn.sigmoid(v)


def _rms(v, g):
    return v * lax.rsqrt(jnp.mean(v * v, axis=-1, keepdims=True) + EPS) * g


def _dot(a, b):
    return jnp.dot(a, b, preferred_element_type=F32)


def _dot_nt(a, b):
    return lax.dot_general(a, b, (((1,), (1,)), ((), ())), preferred_element_type=F32)


def _rope(v, c, s1, s2):
    return v * c + pltpu.roll(v, 16, axis=1) * s1 + pltpu.roll(v, LANES - 16, axis=1) * s2


def _mod_row(ref, row):
    return ref[0, pl.ds(row, 1), :]


def _layer_spec(arr, idx):
    zeros = (0,) * (arr.ndim - 1)
    return pl.BlockSpec((1,) + arr.shape[1:], lambda i, t: (idx,) + zeros)


def _mod_spec(layer, part):
    return pl.BlockSpec((1, MOD_ROWS, D_MODEL), lambda i, t: (layer, 0, part))


def _tok_spec(tm, width):
    return pl.BlockSpec((1, tm, width), lambda i, t: (i, t, 0))


_PARAMS = pltpu.CompilerParams(vmem_limit_bytes=VMEM_LIMIT)


def _mod_kernel(c_ref, w_ref, b_ref, o_ref):
    sc = _silu(c_ref[...]).astype(BF16)
    o_ref[0] = _dot(sc, w_ref[0].astype(BF16)) + b_ref[0]


def _modulation(cvec, w_mod, b_mod):
    d = D_MODEL
    return pl.pallas_call(
        _mod_kernel,
        grid=(DEPTH, 3),
        in_specs=[
            pl.BlockSpec((MOD_ROWS, d), lambda i, j: (0, 0)),
            pl.BlockSpec((1, d, d), lambda i, j: (i, 0, j)),
            pl.BlockSpec((1, 1, d), lambda i, j: (i, 0, j)),
        ],
        out_specs=pl.BlockSpec((1, MOD_ROWS, d), lambda i, j: (i, 0, j)),
        out_shape=jax.ShapeDtypeStruct((DEPTH, MOD_ROWS, 3 * d), F32),
        compiler_params=_PARAMS,
        name="modulation",
    )(cvec, w_mod, b_mod.reshape(DEPTH, 1, 3 * d))


def _mla_proj_kernel(ctx_row, x_ref, shift_ref, scale_ref, gpre_ref, win_ref, gq_ref, wuq_ref, gkv_ref,
                     wuk_ref, wuvt_ref, tc_ref, ts1_ref, ts2_ref,
                     qn_ref, qr_ref, kn_ref, kr_ref, vt_ref, sg_ref):
    row = pl.program_id(0) if ctx_row is None else ctx_row
    x = x_ref[0]
    h = _rms(x, gpre_ref[0]) * (1.0 + _mod_row(scale_ref, row)) + _mod_row(shift_ref, row)
    u = _dot(h.astype(BF16), win_ref[0])
    tc, ts1, ts2 = tc_ref[...], ts1_ref[...], ts2_ref[...]

    cqn = _rms(u[:, _CQ0:_CKV0], gq_ref[0]).astype(BF16)
    q = _dot(cqn, wuq_ref[0])
    for j in range(_QR_COLS // LANES):
        lo = _QN_COLS + j * LANES
        qr_ref[0, j] = _rope(q[:, lo:lo + LANES], tc, ts1, ts2).astype(BF16)

    ckvn = _rms(u[:, _CKV0:_KR0], gkv_ref[0]).astype(BF16)
    kn = _dot(ckvn, wuk_ref[0])
    sg = _silu(u[:, _GATE0:])
    for hd in range(N_HEADS):
        hs = slice(hd * NOPE_DIM, (hd + 1) * NOPE_DIM)
        qn_ref[0, hd] = q[:, hs].astype(BF16)
        kn_ref[0, hd] = kn[:, hs].astype(BF16)
        sg_ref[0, hd] = sg[:, hs].astype(BF16)
    vt_ref[0] = _dot_nt(wuvt_ref[0], ckvn).astype(BF16)
    kr_ref[0] = _rope(u[:, _KR0:_GATE0], tc, ts1, ts2).astype(BF16)


def _mla_proj(x, mod, layer, j, pm, tables, ctx_row, tm):
    b, l, d = x.shape
    tabspec = pl.BlockSpec((tm, LANES), lambda i, t: (t, 0))
    heads = lambda n: pl.BlockSpec((1, n, tm, LANES), lambda i, t: (i, 0, t, 0))
    hshape = lambda n: jax.ShapeDtypeStruct((b, n, l, LANES), BF16)
    nh, npair = N_HEADS, N_HEADS // _HEAD_PAIR
    weights = [pm["w_in"], pm["g_q"], pm["w_uq"], pm["g_kv"], pm["w_uk"], pm["w_uvt"]]
    return pl.pallas_call(
        functools.partial(_mla_proj_kernel, ctx_row),
        grid=(b, l // tm),
        in_specs=[_tok_spec(tm, d), _mod_spec(layer, 0), _mod_spec(layer, 1), _layer_spec(pm["g_pre"], layer)]
                 + [_layer_spec(w, j) for w in weights] + [tabspec, tabspec, tabspec],
        out_specs=[heads(nh), heads(npair), heads(nh), _tok_spec(tm, LANES),
                   pl.BlockSpec((1, ATTN_W, tm), lambda i, t: (i, 0, t)), heads(nh)],
        out_shape=[hshape(nh), hshape(npair), hshape(nh), jax.ShapeDtypeStruct((b, l, LANES), BF16),
                   jax.ShapeDtypeStruct((b, ATTN_W, l), BF16), hshape(nh)],
        compiler_params=_PARAMS,
        name="mla_proj",
    )(x, mod, mod, pm["g_pre"], *weights, *tables)


def _attn_kernel(seg_lens, tq, qn_ref, qr_ref, sg_ref, *rest):
    nseg = len(seg_lens)
    seg_refs = rest[:3 * nseg]
    o_ref, k_scr, vt_scr, st_scr, m_scr, p_scr = rest[3 * nseg:]
    lane_half = lax.broadcasted_iota(jnp.int32, (1, LANES), 1) // ROPE_DIM
    nt = qn_ref.shape[2] // tq
    n_units = _HEAD_PAIR * nt
    assert n_units >= 2

    for hh in range(_HEAD_PAIR):
        off = 0
        for s, ls in enumerate(seg_lens):
            kn_ref, kr_ref, vt_ref = seg_refs[3 * s:3 * s + 3]
            k_scr[hh, off:off + ls, :NOPE_DIM] = kn_ref[0, hh]
            k_scr[hh, off:off + ls, NOPE_DIM:] = jnp.where(lane_half == hh, kr_ref[0], jnp.zeros((), BF16))
            vt_scr[hh, :V_DIM, off:off + ls] = vt_ref[0, hh * V_DIM:(hh + 1) * V_DIM, :]
            off += ls
        vt_scr[hh, V_DIM:, :] = jnp.ones((_ONES_ROWS, off), BF16)

    def unit(u):
        return u // nt, pl.ds(pl.multiple_of((u % nt) * tq, tq), tq)

    def scores(u):
        hh, rows = unit(u)
        q = jnp.concatenate([qn_ref[0, hh, rows, :], qr_ref[0, 0, rows, :]], axis=1)
        st = _dot_nt(k_scr[hh], q)
        st_scr[...] = st
        m_scr[...] = jnp.max(st, axis=0, keepdims=True)

    def probs():
        m = m_scr[...]
        for c in range(0, st_scr.shape[0], _PROB_ROWS):
            rows = slice(c, c + _PROB_ROWS)
            p_scr[rows, :] = jnp.exp2((st_scr[rows, :] - m).astype(BF16))

    def values(u):
        hh, rows = unit(u)
        ot = _dot(vt_scr[hh], p_scr[...])
        ot = ot[:V_DIM] / ot[V_DIM:V_DIM + 1]
        o_ref[0, hh, rows, :] = (ot.T * sg_ref[0, hh, rows, :].astype(F32)).astype(BF16)

    scores(0)
    probs()
    scores(1)

    def iteration(i, carry):
        values(i - 2)
        probs()
        scores(i)
        return carry

    lax.fori_loop(2, n_units, iteration, 0)
    values(n_units - 2)
    probs()
    values(n_units - 1)


def _attention(qn, qr, sg, segments, tq):
    b, nh, lq, _ = qn.shape
    seg_lens = tuple(s[0].shape[2] for s in segments)
    lk = sum(seg_lens)
    hp = _HEAD_PAIR
    pair = lambda l: pl.BlockSpec((1, hp, l, LANES), lambda i, h: (i, h, 0, 0))
    in_specs = [pair(lq), pl.BlockSpec((1, 1, lq, LANES), lambda i, h: (i, h, 0, 0)), pair(lq)]
    args = [qn, qr, sg]
    for kn, kr, vt in segments:
        ls = kn.shape[2]
        in_specs += [pair(ls), pl.BlockSpec((1, ls, LANES), lambda i, h: (i, 0, 0)),
                     pl.BlockSpec((1, hp * V_DIM, ls), lambda i, h: (i, h, 0))]
        args += [kn, kr, vt]
    return pl.pallas_call(
        functools.partial(_attn_kernel, seg_lens, tq),
        grid=(b, nh // hp),
        in_specs=in_specs,
        out_specs=pair(lq),
        out_shape=jax.ShapeDtypeStruct((b, nh, lq, LANES), BF16),
        scratch_shapes=[pltpu.VMEM((hp, lk, NOPE_DIM + LANES), BF16),
                        pltpu.VMEM((hp, V_DIM + _ONES_ROWS, lk), BF16),
                        pltpu.VMEM((lk, tq), F32), pltpu.VMEM((1, tq), F32), pltpu.VMEM((lk, tq), BF16)],
        compiler_params=_PARAMS,
        name="attention",
    )(*args)


def _conv_in_math(x, row, shift_ref, scale_ref, gpre_ref, win_ref, u_ref, sg_ref):
    h = _rms(x, gpre_ref[0]) * (1.0 + _mod_row(scale_ref, row)) + _mod_row(shift_ref, row)
    u3 = _dot(h.astype(BF16), win_ref[0])
    u_ref[0] = (u3[:, :CONV_W] * jax.nn.sigmoid(u3[:, CONV_W:2 * CONV_W])).astype(BF16)
    sg_ref[0] = _silu(u3[:, 2 * CONV_W:]).astype(BF16)


def _out_kernel(ctx_row, fuse_conv_in, og_ref, x_ref, gate_ref, wo_ref, gpost_ref, *rest):
    row = pl.program_id(0) if ctx_row is None else ctx_row
    og = jnp.concatenate([og_ref[0, hd] for hd in range(og_ref.shape[1])], axis=1)
    y = _dot(og, wo_ref[0])
    x_new = x_ref[0] + _mod_row(gate_ref, row) * _rms(y, gpost_ref[0])
    if fuse_conv_in:
        shift_ref, scale_ref, gpre_ref, win_ref, o_ref, u_ref, sg_ref = rest
        _conv_in_math(x_new, row, shift_ref, scale_ref, gpre_ref, win_ref, u_ref, sg_ref)
    else:
        o_ref, = rest
    o_ref[0] = x_new


def _out_residual(og, x, mod, layer, j, pm, ctx_row, tm, next_conv=None):
    b, l, d = x.shape
    in_specs = [pl.BlockSpec((1, og.shape[1], tm, LANES), lambda i, t: (i, 0, t, 0)), _tok_spec(tm, d),
                _mod_spec(layer, 2), _layer_spec(pm["w_o"], j), _layer_spec(pm["g_post"], layer)]
    args = [og, x, mod, pm["w_o"], pm["g_post"]]
    out_specs, out_shape = [_tok_spec(tm, d)], [jax.ShapeDtypeStruct((b, l, d), F32)]
    if next_conv is not None:
        pc, jc = next_conv
        in_specs += [_mod_spec(layer + 1, 0), _mod_spec(layer + 1, 1), _layer_spec(pc["g_pre"], layer + 1),
                     _layer_spec(pc["w_in"], jc)]
        args += [mod, mod, pc["g_pre"], pc["w_in"]]
        out_specs += [_tok_spec(tm, CONV_W)] * 2
        out_shape += [jax.ShapeDtypeStruct((b, l, CONV_W), BF16)] * 2
    res = pl.pallas_call(
        functools.partial(_out_kernel, ctx_row, next_conv is not None),
        grid=(b, l // tm),
        in_specs=in_specs,
        out_specs=out_specs,
        out_shape=out_shape,
        compiler_params=_PARAMS,
        name="out_residual",
    )(*args)
    return res if next_conv is not None else res[0]


def _conv_in_kernel(ctx_row, x_ref, shift_ref, scale_ref, gpre_ref, win_ref, u_ref, sg_ref):
    row = pl.program_id(0) if ctx_row is None else ctx_row
    _conv_in_math(x_ref[0], row, shift_ref, scale_ref, gpre_ref, win_ref, u_ref, sg_ref)


def _conv_in(x, mod, layer, j, pc, ctx_row, tm):
    b, l, d = x.shape
    return pl.pallas_call(
        functools.partial(_conv_in_kernel, ctx_row),
        grid=(b, l // tm),
        in_specs=[_tok_spec(tm, d), _mod_spec(layer, 0), _mod_spec(layer, 1), _layer_spec(pc["g_pre"], layer),
                  _layer_spec(pc["w_in"], j)],
        out_specs=[_tok_spec(tm, CONV_W)] * 2,
        out_shape=[jax.ShapeDtypeStruct((b, l, CONV_W), BF16)] * 2,
        compiler_params=_PARAMS,
        name="conv_in",
    )(x, mod, mod, pc["g_pre"], pc["w_in"])


def _conv_out_kernel(ctx_row, tm, u_ref, sg_ref, x_ref, gate_ref, wdw_ref, bdw_ref, gln_ref, bln_ref,
                     wo_ref, gpost_ref, o_ref, win, cv):
    row = pl.program_id(0) if ctx_row is None else ctx_row
    t = pl.program_id(1)
    nt = pl.num_programs(1)
    l = u_ref.shape[1]
    r0 = pl.multiple_of(t * tm, tm)

    top0 = pl.multiple_of(jnp.maximum(r0 - _HALO, 0), _HALO)
    bot0 = pl.multiple_of(jnp.minimum(r0 + tm, l - _HALO), _HALO)
    top = u_ref[0, pl.ds(top0, _HALO), :].astype(F32) * (t > 0).astype(F32)
    mid = u_ref[0, pl.ds(r0, tm), :].astype(F32)
    bot = u_ref[0, pl.ds(bot0, _HALO), :].astype(F32) * (t < nt - 1).astype(F32)
    for lc in range(CONV_W // LANES):
        ls = slice(lc * LANES, (lc + 1) * LANES)
        win[lc, 0:_HALO] = top[:, ls]
        win[lc, _HALO:_HALO + tm] = mid[:, ls]
        win[lc, _HALO + tm:] = bot[:, ls]

    for rc in range(tm // _CONV_ROWS):
        for lc in range(CONV_W // LANES):
            ls = slice(lc * LANES, (lc + 1) * LANES)
            acc = jnp.broadcast_to(bdw_ref[0, :, ls], (_CONV_ROWS, LANES))
            for k in range(CONV_K):
                lo = rc * _CONV_ROWS + k + _HALO - CONV_PAD
                acc = acc + win[lc, lo:lo + _CONV_ROWS, :] * wdw_ref[0, k:k + 1, ls]
            cv[rc * _CONV_ROWS:(rc + 1) * _CONV_ROWS, ls] = acc

    c = cv[...]
    mu = jnp.mean(c, axis=-1, keepdims=True)
    cc = c - mu
    var = jnp.mean(cc * cc, axis=-1, keepdims=True)
    ln = cc * lax.rsqrt(var + EPS) * gln_ref[0] + bln_ref[0]
    z = (_silu(ln) * sg_ref[0].astype(F32)).astype(BF16)
    y = _dot(z, wo_ref[0])
    o_ref[0] = x_ref[0] + _mod_row(gate_ref, row) * _rms(y, gpost_ref[0])


def _conv_out(u, sg, x, mod, layer, j, pc, ctx_row, tm):
    b, l, d = x.shape
    per_conv = [pc["w_dw"], pc["b_dw"], pc["g_ln"], pc["b_ln"], pc["w_o"]]
    return pl.pallas_call(
        functools.partial(_conv_out_kernel, ctx_row, tm),
        grid=(b, l // tm),
        in_specs=[pl.BlockSpec((1, l, CONV_W), lambda i, t: (i, 0, 0)), _tok_spec(tm, CONV_W), _tok_spec(tm, d),
                  _mod_spec(layer, 2)] + [_layer_spec(w, j) for w in per_conv] + [_layer_spec(pc["g_post"], layer)],
        out_specs=_tok_spec(tm, d),
        out_shape=jax.ShapeDtypeStruct((b, l, d), F32),
        scratch_shapes=[pltpu.VMEM((CONV_W // LANES, tm + 2 * _HALO, LANES), F32), pltpu.VMEM((tm, CONV_W), F32)],
        compiler_params=_PARAMS,
        name="conv_out",
    )(u, sg, x, mod, *per_conv, pc["g_post"])


def _rope_tables(n_tokens):
    f32 = np.float32
    rows_n = n_tokens // GRID_W
    rows = np.repeat(np.arange(rows_n, dtype=f32), GRID_W)
    cols = np.tile(np.arange(GRID_W, dtype=f32), rows_n)
    half = ROPE_DIM // 2
    freqs = (f32(1.0) / (f32(ROPE_BASE) ** (np.arange(0, half, 2, dtype=f32) / f32(half)))).astype(f32)
    ang_r = rows[:, None] * freqs[None, :]
    ang_c = cols[:, None] * freqs[None, :]
    zero = np.zeros_like(ang_r)
    cos64 = np.concatenate([np.cos(ang_r), np.cos(ang_r), np.cos(ang_c), np.cos(ang_c)], axis=-1)
    s1_64 = np.concatenate([zero, np.sin(ang_r), zero, np.sin(ang_c)], axis=-1)
    s2_64 = np.concatenate([-np.sin(ang_r), zero, -np.sin(ang_c), zero], axis=-1)
    dup = lambda a: jnp.asarray(np.concatenate([a, a], axis=-1).astype(f32))
    return dup(cos64), dup(s1_64), dup(s2_64)


def _identity_tables(n_tokens):
    one = jnp.asarray(np.ones((n_tokens, LANES), np.float32))
    zero = jnp.asarray(np.zeros((n_tokens, LANES), np.float32))
    return one, zero, zero


def _mla_params(g_pre, g_post, mla_w_in, mla_g_q, mla_w_uq, mla_g_kv, mla_w_ukv, mla_w_o):
    n = mla_w_in.shape[0]
    kr = mla_w_in[:, :, _KR0:_KR0 + ROPE_DIM]
    w_in = jnp.concatenate([mla_w_in[:, :, :_KR0], kr, kr, mla_w_in[:, :, _KR0 + ROPE_DIM:]], axis=2).astype(BF16)
    w_uq = mla_w_uq.reshape(n, Q_LORA, N_HEADS, NOPE_DIM + ROPE_DIM)
    w_uq = jnp.concatenate([w_uq[..., :NOPE_DIM].reshape(n, Q_LORA, _QN_COLS),
                            w_uq[..., NOPE_DIM:].reshape(n, Q_LORA, _QR_COLS)], axis=2).astype(BF16)
    w_ukv = mla_w_ukv.reshape(n, KV_LORA, N_HEADS, NOPE_DIM + V_DIM)
    w_uk = w_ukv[..., :NOPE_DIM].reshape(n, KV_LORA, _QN_COLS).astype(BF16)
    w_uvt = jnp.swapaxes(w_ukv[..., NOPE_DIM:].reshape(n, KV_LORA, ATTN_W), 1, 2).astype(BF16)
    scale = math.log2(math.e) / math.sqrt(NOPE_DIM + ROPE_DIM)
    return dict(g_pre=g_pre, g_post=g_post, w_in=w_in, g_q=(mla_g_q * scale)[:, None, :], w_uq=w_uq,
                g_kv=mla_g_kv[:, None, :], w_uk=w_uk, w_uvt=w_uvt, w_o=mla_w_o.astype(BF16))


def kernel(x, c, ctx, c_ctx, w_mod, b_mod, g_pre, g_post, mla_w_in, mla_g_q, mla_w_uq, mla_g_kv, mla_w_ukv,
           mla_w_o, cv_w_in, cv_w_dw, cv_b_dw, cv_g_ln, cv_b_ln, cv_w_o):
    b, s, d = x.shape
    lc = ctx.shape[1]
    ctx_row = b
    assert b < MOD_ROWS and s % GRID_W == 0

    cvec = jnp.zeros((MOD_ROWS, d), F32).at[:b].set(c).at[ctx_row].set(c_ctx)
    mod = _modulation(cvec, w_mod, b_mod)

    tables_x = _rope_tables(s)
    tables_c = _identity_tables(lc)
    g_pre3, g_post3 = g_pre[:, None, :], g_post[:, None, :]
    pm = _mla_params(g_pre3, g_post3, mla_w_in, mla_g_q, mla_w_uq, mla_g_kv, mla_w_ukv, mla_w_o)
    pc = dict(g_pre=g_pre3, g_post=g_post3, w_in=cv_w_in.astype(BF16), w_dw=cv_w_dw.reshape(-1, CONV_K, CONV_W),
              b_dw=cv_b_dw[:, None, :], g_ln=cv_g_ln[:, None, :], b_ln=cv_b_ln[:, None, :], w_o=cv_w_o.astype(BF16))

    conv_inputs = None
    for i in range(DEPTH):
        last = i == DEPTH - 1
        j = i // N_MIXERS
        if i % N_MIXERS == 0:
            next_conv = (pc, (i + 1) // N_MIXERS) if not last and (i + 1) % N_MIXERS == 1 else None
            qn, qr, kn, kr, vt, sg = _mla_proj(x, mod, i, j, pm, tables_x, None, _TM_PROJ)
            qn_c, qr_c, kn_c, kr_c, vt_c, sg_c = _mla_proj(ctx, mod, i, j, pm, tables_c, ctx_row, lc)
            og = _attention(qn, qr, sg, [(kn, kr, vt), (kn_c, kr_c, vt_c)], _TQ)
            res_x = _out_residual(og, x, mod, i, j, pm, None, _TM_PROJ, next_conv)
            res_c = None
            if not last:
                og_c = _attention(qn_c, qr_c, sg_c, [(kn_c, kr_c, vt_c)], lc)
                res_c = _out_residual(og_c, ctx, mod, i, j, pm, ctx_row, lc, next_conv)
            if next_conv is None:
                x, ctx = res_x, (ctx if res_c is None else res_c)
            else:
                x, ctx = res_x[0], res_c[0]
                conv_inputs = (res_x[1:], res_c[1:])
        else:
            if conv_inputs is None:
                conv_inputs = (_conv_in(x, mod, i, j, pc, None, _TM_PROJ),
                               _conv_in(ctx, mod, i, j, pc, ctx_row, lc) if not last else None)
            (u, sg), uc_sgc = conv_inputs
            conv_inputs = None
            x = _conv_out(u, sg, x, mod, i, j, pc, None, _TM_CONV)
            if not last:
                ctx = _conv_out(uc_sgc[0], uc_sgc[1], ctx, mod, i, j, pc, ctx_row, lc)
    return x
```

```python
import functools
import math

import jax
import jax.numpy as jnp
import numpy as np
from jax import lax
from jax.experimental import pallas as pl
from jax.experimental.pallas import tpu as pltpu

D_MODEL = 1024
DEPTH = 4
GRID_W = 64
N_MIXERS = 2
N_HEADS = 8
NOPE_DIM = 128
ROPE_DIM = 64
V_DIM = 128
Q_LORA = 384
KV_LORA = 256
ATTN_W = N_HEADS * V_DIM
ROPE_BASE = 10000.0
CONV_W = D_MODEL
CONV_K = 31
CONV_PAD = CONV_K // 2
EPS = 1e-6

LANES = 128
MOD_ROWS = 16
VMEM_LIMIT = 56 * 1024 * 1024
_HEAD_PAIR = 2
_HEADS_X = 2
_ONES_ROWS = 16
_PROB_ROWS = 64
_TM_PROJ = 512
_TM_CONV = 256
_TILE_PARTS = 2
_TQ = 512
_HALO = 16
_CONV_ROWS = 128
assert _HALO >= CONV_PAD + 1

F32 = jnp.float32
BF16 = jnp.bfloat16

_CQ0, _CKV0, _KR0, _GATE0 = 0, Q_LORA, Q_LORA + KV_LORA, Q_LORA + KV_LORA + LANES
_WIN_COLS = _GATE0 + ATTN_W
_QN_COLS = N_HEADS * NOPE_DIM
_QR_COLS = N_HEADS * ROPE_DIM


def _silu(v):
    return v * jax.nn.sigmoid(v)


def _rms(v, g):
    return v * lax.rsqrt(jnp.mean(v * v, axis=-1, keepdims=True) + EPS) * g


def _dot(a, b):
    return jnp.dot(a, b, preferred_element_type=F32)


def _dot_nt(a, b):
    return lax.dot_general(a, b, (((1,), (1,)), ((), ())), preferred_element_type=F32)


def _rope(v, c, s1, s2):
    return v * c + pltpu.roll(v, 16, axis=1) * s1 + pltpu.roll(v, LANES - 16, axis=1) * s2


def _mod_row(ref, row):
    return ref[0, pl.ds(row, 1), :]


def _layer_spec(arr, idx):
    zeros = (0,) * (arr.ndim - 1)
    return pl.BlockSpec((1,) + arr.shape[1:], lambda i, t: (idx,) + zeros)


def _mod_spec(layer, part):
    return pl.BlockSpec((1, MOD_ROWS, D_MODEL), lambda i, t: (layer, 0, part))


def _tok_spec(tm, width):
    return pl.BlockSpec((1, tm, width), lambda i, t: (i, t, 0))


_PARAMS = pltpu.CompilerParams(vmem_limit_bytes=VMEM_LIMIT)


def _mod_kernel(c_ref, w_ref, b_ref, o_ref):
    sc = _silu(c_ref[...]).astype(BF16)
    o_ref[0] = _dot(sc, w_ref[0].astype(BF16)) + b_ref[0]


def _modulation(cvec, w_mod, b_mod):
    d = D_MODEL
    return pl.pallas_call(
        _mod_kernel,
        grid=(DEPTH, 3),
        in_specs=[
            pl.BlockSpec((MOD_ROWS, d), lambda i, j: (0, 0)),
            pl.BlockSpec((1, d, d), lambda i, j: (i, 0, j)),
            pl.BlockSpec((1, 1, d), lambda i, j: (i, 0, j)),
        ],
        out_specs=pl.BlockSpec((1, MOD_ROWS, d), lambda i, j: (i, 0, j)),
        out_shape=jax.ShapeDtypeStruct((DEPTH, MOD_ROWS, 3 * d), F32),
        compiler_params=_PARAMS,
        name="modulation",
    )(cvec, w_mod, b_mod.reshape(DEPTH, 1, 3 * d))


def _mla_proj_kernel(ctx_row, x_ref, shift_ref, scale_ref, gpre_ref, win_ref, gq_ref, wuq_ref, gkv_ref,
                     wuk_ref, wuvt_ref, *rest):
    row = pl.program_id(0) if ctx_row is None else ctx_row
    if ctx_row is None:
        tc_ref, ts1_ref, ts2_ref, qn_ref, qr_ref, kn_ref, kr_ref, vt_ref, sg_ref = rest
        tc, ts1, ts2 = tc_ref[...], ts1_ref[...], ts2_ref[...]
        rope = lambda v: _rope(v, tc, ts1, ts2)
    else:
        qn_ref, qr_ref, kn_ref, kr_ref, vt_ref, sg_ref = rest
        rope = lambda v: v
    x = x_ref[0]
    h = _rms(x, gpre_ref[0]) * (1.0 + _mod_row(scale_ref, row)) + _mod_row(shift_ref, row)
    u = _dot(h.astype(BF16), win_ref[0])

    cqn = _rms(u[:, _CQ0:_CKV0], gq_ref[0]).astype(BF16)
    q = _dot(cqn, wuq_ref[0])
    for j in range(_QR_COLS // LANES):
        lo = _QN_COLS + j * LANES
        qr_ref[0, j] = rope(q[:, lo:lo + LANES]).astype(BF16)

    ckvn = _rms(u[:, _CKV0:_KR0], gkv_ref[0]).astype(BF16)
    kn = _dot(ckvn, wuk_ref[0])
    sg = _silu(u[:, _GATE0:])
    for hd in range(N_HEADS):
        hs = slice(hd * NOPE_DIM, (hd + 1) * NOPE_DIM)
        qn_ref[0, hd] = q[:, hs].astype(BF16)
        kn_ref[0, hd] = kn[:, hs].astype(BF16)
        sg_ref[0, hd] = sg[:, hs].astype(BF16)
    vt_ref[0] = _dot_nt(wuvt_ref[0], ckvn).astype(BF16)
    kr_ref[0] = rope(u[:, _KR0:_GATE0]).astype(BF16)


def _mla_proj(x, mod, layer, j, pm, tables, ctx_row, tm):
    b, l, d = x.shape
    assert (ctx_row is None) == (len(tables) == 3)
    tabspec = pl.BlockSpec((tm, LANES), lambda i, t: (t, 0))
    heads = lambda n: pl.BlockSpec((1, n, tm, LANES), lambda i, t: (i, 0, t, 0))
    hshape = lambda n: jax.ShapeDtypeStruct((b, n, l, LANES), BF16)
    nh, npair = N_HEADS, N_HEADS // _HEAD_PAIR
    weights = [pm["w_in"], pm["g_q"], pm["w_uq"], pm["g_kv"], pm["w_uk"], pm["w_uvt"]]
    return pl.pallas_call(
        functools.partial(_mla_proj_kernel, ctx_row),
        grid=(b, l // tm),
        in_specs=[_tok_spec(tm, d), _mod_spec(layer, 0), _mod_spec(layer, 1), _layer_spec(pm["g_pre"], layer)]
                 + [_layer_spec(w, j) for w in weights] + [tabspec] * len(tables),
        out_specs=[heads(nh), heads(npair), heads(nh), _tok_spec(tm, LANES),
                   pl.BlockSpec((1, ATTN_W, tm), lambda i, t: (i, 0, t)), heads(nh)],
        out_shape=[hshape(nh), hshape(npair), hshape(nh), jax.ShapeDtypeStruct((b, l, LANES), BF16),
                   jax.ShapeDtypeStruct((b, ATTN_W, l), BF16), hshape(nh)],
        compiler_params=_PARAMS,
        name="mla_proj",
    )(x, mod, mod, pm["g_pre"], *weights, *tables)


def _attn_kernel(seg_lens, tq, qn_ref, qr_ref, sg_ref, *rest):
    nseg = len(seg_lens)
    seg_refs = rest[:3 * nseg]
    o_ref, k_scr, vt_scr, st_scr, m_scr, p_scr = rest[3 * nseg:]
    lane_half = lax.broadcasted_iota(jnp.int32, (1, LANES), 1) // ROPE_DIM
    hp = qn_ref.shape[1]
    nt = qn_ref.shape[2] // tq
    n_units = hp * nt
    assert n_units >= 2

    for hh in range(hp):
        off = 0
        for s, ls in enumerate(seg_lens):
            kn_ref, kr_ref, vt_ref = seg_refs[3 * s:3 * s + 3]
            k_scr[hh, off:off + ls, :NOPE_DIM] = kn_ref[0, hh]
            in_half = lane_half == hh % _HEAD_PAIR
            k_scr[hh, off:off + ls, NOPE_DIM:] = jnp.where(in_half, kr_ref[0], jnp.zeros((), BF16))
            vt_scr[hh, :V_DIM, off:off + ls] = vt_ref[0, hh * V_DIM:(hh + 1) * V_DIM, :]
            off += ls
        vt_scr[hh, V_DIM:, :] = jnp.ones((_ONES_ROWS, off), BF16)

    def unit(u):
        return u // nt, pl.ds(pl.multiple_of((u % nt) * tq, tq), tq)

    def scores(u):
        hh, rows = unit(u)
        q = jnp.concatenate([qn_ref[0, hh, rows, :], qr_ref[0, hh // _HEAD_PAIR, rows, :]], axis=1)
        st = _dot_nt(k_scr[hh], q)
        st_scr[...] = st
        m_scr[...] = jnp.max(st, axis=0, keepdims=True)

    def probs():
        m = m_scr[...]
        for c in range(0, st_scr.shape[0], _PROB_ROWS):
            rows = slice(c, c + _PROB_ROWS)
            p_scr[rows, :] = jnp.exp2((st_scr[rows, :] - m).astype(BF16))

    def values(u):
        hh, rows = unit(u)
        ot = _dot(vt_scr[hh], p_scr[...])
        ot = ot[:V_DIM] / ot[V_DIM:V_DIM + 1]
        o_ref[0, hh, rows, :] = (ot.T * sg_ref[0, hh, rows, :].astype(F32)).astype(BF16)

    scores(0)
    probs()
    scores(1)

    def iteration(i, carry):
        values(i - 2)
        probs()
        scores(i)
        return carry

    lax.fori_loop(2, n_units, iteration, 0)
    values(n_units - 2)
    probs()
    values(n_units - 1)


def _attention(qn, qr, sg, segments, tq, hp):
    b, nh, lq, _ = qn.shape
    seg_lens = tuple(s[0].shape[2] for s in segments)
    lk = sum(seg_lens)
    assert hp % _HEAD_PAIR == 0 and nh % hp == 0
    pair = lambda l: pl.BlockSpec((1, hp, l, LANES), lambda i, h: (i, h, 0, 0))
    in_specs = [pair(lq), pl.BlockSpec((1, hp // _HEAD_PAIR, lq, LANES), lambda i, h: (i, h, 0, 0)), pair(lq)]
    args = [qn, qr, sg]
    for kn, kr, vt in segments:
        ls = kn.shape[2]
        in_specs += [pair(ls), pl.BlockSpec((1, ls, LANES), lambda i, h: (i, 0, 0)),
                     pl.BlockSpec((1, hp * V_DIM, ls), lambda i, h: (i, h, 0))]
        args += [kn, kr, vt]
    return pl.pallas_call(
        functools.partial(_attn_kernel, seg_lens, tq),
        grid=(b, nh // hp),
        in_specs=in_specs,
        out_specs=pair(lq),
        out_shape=jax.ShapeDtypeStruct((b, nh, lq, LANES), BF16),
        scratch_shapes=[pltpu.VMEM((hp, lk, NOPE_DIM + LANES), BF16),
                        pltpu.VMEM((hp, V_DIM + _ONES_ROWS, lk), BF16),
                        pltpu.VMEM((lk, tq), F32), pltpu.VMEM((1, tq), F32), pltpu.VMEM((lk, tq), BF16)],
        compiler_params=_PARAMS,
        name="attention",
    )(*args)


def _conv_in_math(x, row, rows, shift_ref, scale_ref, gpre_ref, win_ref, u_ref, sg_ref):
    h = _rms(x, gpre_ref[0]) * (1.0 + _mod_row(scale_ref, row)) + _mod_row(shift_ref, row)
    u3 = _dot(h.astype(BF16), win_ref[0])
    u_ref[0, rows, :] = (u3[:, :CONV_W] * jax.nn.sigmoid(u3[:, CONV_W:2 * CONV_W])).astype(BF16)
    sg_ref[0, rows, :] = _silu(u3[:, 2 * CONV_W:]).astype(BF16)


def _row_parts(tm):
    n = _TILE_PARTS if tm % (_TILE_PARTS * LANES) == 0 else 1
    return [slice(i * (tm // n), (i + 1) * (tm // n)) for i in range(n)]


def _out_kernel(ctx_row, fuse_conv_in, og_ref, x_ref, gate_ref, wo_ref, gpost_ref, *rest):
    row = pl.program_id(0) if ctx_row is None else ctx_row
    parts = _row_parts(x_ref.shape[1])
    ys = []
    for rows in parts:
        og = jnp.concatenate([og_ref[0, hd, rows, :] for hd in range(og_ref.shape[1])], axis=1)
        ys.append(_dot(og, wo_ref[0]))
    for rows, y in zip(parts, ys):
        x_new = x_ref[0, rows, :] + _mod_row(gate_ref, row) * _rms(y, gpost_ref[0])
        if fuse_conv_in:
            shift_ref, scale_ref, gpre_ref, win_ref, o_ref, u_ref, sg_ref = rest
            _conv_in_math(x_new, row, rows, shift_ref, scale_ref, gpre_ref, win_ref, u_ref, sg_ref)
        else:
            o_ref, = rest
        o_ref[0, rows, :] = x_new


def _out_residual(og, x, mod, layer, j, pm, ctx_row, tm, next_conv=None):
    b, l, d = x.shape
    in_specs = [pl.BlockSpec((1, og.shape[1], tm, LANES), lambda i, t: (i, 0, t, 0)), _tok_spec(tm, d),
                _mod_spec(layer, 2), _layer_spec(pm["w_o"], j), _layer_spec(pm["g_post"], layer)]
    args = [og, x, mod, pm["w_o"], pm["g_post"]]
    out_specs, out_shape = [_tok_spec(tm, d)], [jax.ShapeDtypeStruct((b, l, d), F32)]
    if next_conv is not None:
        pc, jc = next_conv
        in_specs += [_mod_spec(layer + 1, 0), _mod_spec(layer + 1, 1), _layer_spec(pc["g_pre"], layer + 1),
                     _layer_spec(pc["w_in"], jc)]
        args += [mod, mod, pc["g_pre"], pc["w_in"]]
        out_specs += [_tok_spec(tm, CONV_W)] * 2
        out_shape += [jax.ShapeDtypeStruct((b, l, CONV_W), BF16)] * 2
    res = pl.pallas_call(
        functools.partial(_out_kernel, ctx_row, next_conv is not None),
        grid=(b, l // tm),
        in_specs=in_specs,
        out_specs=out_specs,
        out_shape=out_shape,
        compiler_params=_PARAMS,
        name="out_residual",
    )(*args)
    return res if next_conv is not None else res[0]


def _conv_in_kernel(ctx_row, x_ref, shift_ref, scale_ref, gpre_ref, win_ref, u_ref, sg_ref):
    row = pl.program_id(0) if ctx_row is None else ctx_row
    for rows in _row_parts(x_ref.shape[1]):
        _conv_in_math(x_ref[0, rows, :], row, rows, shift_ref, scale_ref, gpre_ref, win_ref, u_ref, sg_ref)


def _conv_in(x, mod, layer, j, pc, ctx_row, tm):
    b, l, d = x.shape
    return pl.pallas_call(
        functools.partial(_conv_in_kernel, ctx_row),
        grid=(b, l // tm),
        in_specs=[_tok_spec(tm, d), _mod_spec(layer, 0), _mod_spec(layer, 1), _layer_spec(pc["g_pre"], layer),
                  _layer_spec(pc["w_in"], j)],
        out_specs=[_tok_spec(tm, CONV_W)] * 2,
        out_shape=[jax.ShapeDtypeStruct((b, l, CONV_W), BF16)] * 2,
        compiler_params=_PARAMS,
        name="conv_in",
    )(x, mod, mod, pc["g_pre"], pc["w_in"])


def _conv_out_kernel(ctx_row, tm, u_ref, sg_ref, x_ref, gate_ref, wdw_ref, bdw_ref, gln_ref, bln_ref,
                     wo_ref, gpost_ref, o_ref, win, cv):
    row = pl.program_id(0) if ctx_row is None else ctx_row
    t = pl.program_id(1)
    nt = pl.num_programs(1)
    l = u_ref.shape[1]
    r0 = pl.multiple_of(t * tm, tm)

    top0 = pl.multiple_of(jnp.maximum(r0 - _HALO, 0), _HALO)
    bot0 = pl.multiple_of(jnp.minimum(r0 + tm, l - _HALO), _HALO)
    top = u_ref[0, pl.ds(top0, _HALO), :].astype(F32) * (t > 0).astype(F32)
    mid = u_ref[0, pl.ds(r0, tm), :].astype(F32)
    bot = u_ref[0, pl.ds(bot0, _HALO), :].astype(F32) * (t < nt - 1).astype(F32)
    for lc in range(CONV_W // LANES):
        ls = slice(lc * LANES, (lc + 1) * LANES)
        win[lc, 0:_HALO] = top[:, ls]
        win[lc, _HALO:_HALO + tm] = mid[:, ls]
        win[lc, _HALO + tm:] = bot[:, ls]

    for rc in range(tm // _CONV_ROWS):
        for lc in range(CONV_W // LANES):
            ls = slice(lc * LANES, (lc + 1) * LANES)
            acc = jnp.broadcast_to(bdw_ref[0, :, ls], (_CONV_ROWS, LANES))
            for k in range(CONV_K):
                lo = rc * _CONV_ROWS + k + _HALO - CONV_PAD
                acc = acc + win[lc, lo:lo + _CONV_ROWS, :] * wdw_ref[0, k:k + 1, ls]
            cv[rc * _CONV_ROWS:(rc + 1) * _CONV_ROWS, ls] = acc

    c = cv[...]
    mu = jnp.mean(c, axis=-1, keepdims=True)
    cc = c - mu
    var = jnp.mean(cc * cc, axis=-1, keepdims=True)
    ln = cc * lax.rsqrt(var + EPS) * gln_ref[0] + bln_ref[0]
    z = (_silu(ln) * sg_ref[0].astype(F32)).astype(BF16)
    y = _dot(z, wo_ref[0])
    o_ref[0] = x_ref[0] + _mod_row(gate_ref, row) * _rms(y, gpost_ref[0])


def _conv_out(u, sg, x, mod, layer, j, pc, ctx_row, tm):
    b, l, d = x.shape
    per_conv = [pc["w_dw"], pc["b_dw"], pc["g_ln"], pc["b_ln"], pc["w_o"]]
    return pl.pallas_call(
        functools.partial(_conv_out_kernel, ctx_row, tm),
        grid=(b, l // tm),
        in_specs=[pl.BlockSpec((1, l, CONV_W), lambda i, t: (i, 0, 0)), _tok_spec(tm, CONV_W), _tok_spec(tm, d),
                  _mod_spec(layer, 2)] + [_layer_spec(w, j) for w in per_conv] + [_layer_spec(pc["g_post"], layer)],
        out_specs=_tok_spec(tm, d),
        out_shape=jax.ShapeDtypeStruct((b, l, d), F32),
        scratch_shapes=[pltpu.VMEM((CONV_W // LANES, tm + 2 * _HALO, LANES), F32), pltpu.VMEM((tm, CONV_W), F32)],
        compiler_params=_PARAMS,
        name="conv_out",
    )(u, sg, x, mod, *per_conv, pc["g_post"])


def _rope_tables(n_tokens):
    f32 = np.float32
    rows_n = n_tokens // GRID_W
    rows = np.repeat(np.arange(rows_n, dtype=f32), GRID_W)
    cols = np.tile(np.arange(GRID_W, dtype=f32), rows_n)
    half = ROPE_DIM // 2
    freqs = (f32(1.0) / (f32(ROPE_BASE) ** (np.arange(0, half, 2, dtype=f32) / f32(half)))).astype(f32)
    ang_r = rows[:, None] * freqs[None, :]
    ang_c = cols[:, None] * freqs[None, :]
    zero = np.zeros_like(ang_r)
    cos64 = np.concatenate([np.cos(ang_r), np.cos(ang_r), np.cos(ang_c), np.cos(ang_c)], axis=-1)
    s1_64 = np.concatenate([zero, np.sin(ang_r), zero, np.sin(ang_c)], axis=-1)
    s2_64 = np.concatenate([-np.sin(ang_r), zero, -np.sin(ang_c), zero], axis=-1)
    dup = lambda a: jnp.asarray(np.concatenate([a, a], axis=-1).astype(f32))
    return dup(cos64), dup(s1_64), dup(s2_64)


def _mla_params(g_pre, g_post, mla_w_in, mla_g_q, mla_w_uq, mla_g_kv, mla_w_ukv, mla_w_o):
    n = mla_w_in.shape[0]
    kr = mla_w_in[:, :, _KR0:_KR0 + ROPE_DIM]
    w_in = jnp.concatenate([mla_w_in[:, :, :_KR0], kr, kr, mla_w_in[:, :, _KR0 + ROPE_DIM:]], axis=2).astype(BF16)
    w_uq = mla_w_uq.reshape(n, Q_LORA, N_HEADS, NOPE_DIM + ROPE_DIM)
    w_uq = jnp.concatenate([w_uq[..., :NOPE_DIM].reshape(n, Q_LORA, _QN_COLS),
                            w_uq[..., NOPE_DIM:].reshape(n, Q_LORA, _QR_COLS)], axis=2).astype(BF16)
    w_ukv = mla_w_ukv.reshape(n, KV_LORA, N_HEADS, NOPE_DIM + V_DIM)
    w_uk = w_ukv[..., :NOPE_DIM].reshape(n, KV_LORA, _QN_COLS).astype(BF16)
    w_uvt = jnp.swapaxes(w_ukv[..., NOPE_DIM:].reshape(n, KV_LORA, ATTN_W), 1, 2).astype(BF16)
    scale = math.log2(math.e) / math.sqrt(NOPE_DIM + ROPE_DIM)
    return dict(g_pre=g_pre, g_post=g_post, w_in=w_in, g_q=(mla_g_q * scale)[:, None, :], w_uq=w_uq,
                g_kv=mla_g_kv[:, None, :], w_uk=w_uk, w_uvt=w_uvt, w_o=mla_w_o.astype(BF16))


def kernel(x, c, ctx, c_ctx, w_mod, b_mod, g_pre, g_post, mla_w_in, mla_g_q, mla_w_uq, mla_g_kv, mla_w_ukv,
           mla_w_o, cv_w_in, cv_w_dw, cv_b_dw, cv_g_ln, cv_b_ln, cv_w_o):
    b, s, d = x.shape
    lc = ctx.shape[1]
    ctx_row = b
    assert b < MOD_ROWS and s % GRID_W == 0

    cvec = jnp.zeros((MOD_ROWS, d), F32).at[:b].set(c).at[ctx_row].set(c_ctx)
    mod = _modulation(cvec, w_mod, b_mod)

    tables_x = _rope_tables(s)
    g_pre3, g_post3 = g_pre[:, None, :], g_post[:, None, :]
    pm = _mla_params(g_pre3, g_post3, mla_w_in, mla_g_q, mla_w_uq, mla_g_kv, mla_w_ukv, mla_w_o)
    pc = dict(g_pre=g_pre3, g_post=g_post3, w_in=cv_w_in.astype(BF16), w_dw=cv_w_dw.reshape(-1, CONV_K, CONV_W),
              b_dw=cv_b_dw[:, None, :], g_ln=cv_g_ln[:, None, :], b_ln=cv_b_ln[:, None, :], w_o=cv_w_o.astype(BF16))

    conv_inputs = None
    for i in range(DEPTH):
        last = i == DEPTH - 1
        j = i // N_MIXERS
        if i % N_MIXERS == 0:
            next_conv = (pc, (i + 1) // N_MIXERS) if not last and (i + 1) % N_MIXERS == 1 else None
            qn, qr, kn, kr, vt, sg = _mla_proj(x, mod, i, j, pm, tables_x, None, _TM_PROJ)
            qn_c, qr_c, kn_c, kr_c, vt_c, sg_c = _mla_proj(ctx, mod, i, j, pm, (), ctx_row, lc)
            og = _attention(qn, qr, sg, [(kn, kr, vt), (kn_c, kr_c, vt_c)], _TQ, _HEADS_X)
            res_x = _out_residual(og, x, mod, i, j, pm, None, _TM_PROJ, next_conv)
            res_c = None
            if not last:
                og_c = _attention(qn_c, qr_c, sg_c, [(kn_c, kr_c, vt_c)], lc, N_HEADS)
                res_c = _out_residual(og_c, ctx, mod, i, j, pm, ctx_row, lc, next_conv)
            if next_conv is None:
                x, ctx = res_x, (ctx if res_c is None else res_c)
            else:
                x, ctx = res_x[0], res_c[0]
                conv_inputs = (res_x[1:], res_c[1:])
        else:
            if conv_inputs is None:
                conv_inputs = (_conv_in(x, mod, i, j, pc, None, _TM_PROJ),
                               _conv_in(ctx, mod, i, j, pc, ctx_row, lc) if not last else None)
            (u, sg), uc_sgc = conv_inputs
            conv_inputs = None
            x = _conv_out(u, sg, x, mod, i, j, pc, None, _TM_CONV)
            if not last:
                ctx = _conv_out(uc_sgc[0], uc_sgc[1], ctx, mod, i, j, pc, ctx_row, lc)
    return x
```

```python
import functools
import math

import jax
import jax.numpy as jnp
import numpy as np
from jax import lax
from jax.experimental import pallas as pl
from jax.experimental.pallas import tpu as pltpu

D_MODEL = 1024
DEPTH = 4
GRID_W = 64
N_MIXERS = 2
N_HEADS = 8
NOPE_DIM = 128
ROPE_DIM = 64
V_DIM = 128
Q_LORA = 384
KV_LORA = 256
ATTN_W = N_HEADS * V_DIM
ROPE_BASE = 10000.0
CONV_W = D_MODEL
CONV_K = 31
CONV_PAD = CONV_K // 2
EPS = 1e-6

LANES = 128
MOD_ROWS = 16
VMEM_LIMIT = 56 * 1024 * 1024
_HEAD_PAIR = 2
_HEADS_X = 2
_ONES_ROWS = 16
_PROB_ROWS = 64
_TM_PROJ = 512
_TM_CONV = 1024
_TILE_PARTS = 2
_TQ = 512
_HALO = 16
_CONV_ROWS = 128
assert _HALO >= CONV_PAD + 1

F32 = jnp.float32
BF16 = jnp.bfloat16

_CQ0, _CKV0, _KR0, _GATE0 = 0, Q_LORA, Q_LORA + KV_LORA, Q_LORA + KV_LORA + LANES
_WIN_COLS = _GATE0 + ATTN_W
_QN_COLS = N_HEADS * NOPE_DIM
_QR_COLS = N_HEADS * ROPE_DIM


def _silu(v):
    return v * jax.nn.sigmoid(v)


def _rms(v, g):
    return v * lax.rsqrt(jnp.mean(v * v, axis=-1, keepdims=True) + EPS) * g


def _dot(a, b):
    return jnp.dot(a, b, preferred_element_type=F32)


def _dot_nt(a, b):
    return lax.dot_general(a, b, (((1,), (1,)), ((), ())), preferred_element_type=F32)


def _rope(v, c, s1, s2):
    return v * c + pltpu.roll(v, 16, axis=1) * s1 + pltpu.roll(v, LANES - 16, axis=1) * s2


def _mod_row(ref, row):
    return ref[0, pl.ds(row, 1), :]


def _layer_spec(arr, idx):
    zeros = (0,) * (arr.ndim - 1)
    return pl.BlockSpec((1,) + arr.shape[1:], lambda i, t: (idx,) + zeros)


def _mod_spec(layer, part):
    return pl.BlockSpec((1, MOD_ROWS, D_MODEL), lambda i, t: (layer, 0, part))


def _tok_spec(tm, width):
    return pl.BlockSpec((1, tm, width), lambda i, t: (i, t, 0))


_PARAMS = pltpu.CompilerParams(vmem_limit_bytes=VMEM_LIMIT)


def _mod_kernel(c_ref, w_ref, b_ref, o_ref):
    sc = _silu(c_ref[...]).astype(BF16)
    o_ref[0] = _dot(sc, w_ref[0].astype(BF16)) + b_ref[0]


def _modulation(cvec, w_mod, b_mod):
    d = D_MODEL
    return pl.pallas_call(
        _mod_kernel,
        grid=(DEPTH,),
        in_specs=[
            pl.BlockSpec((MOD_ROWS, d), lambda i: (0, 0)),
            pl.BlockSpec((1, d, 3 * d), lambda i: (i, 0, 0)),
            pl.BlockSpec((1, 1, 3 * d), lambda i: (i, 0, 0)),
        ],
        out_specs=pl.BlockSpec((1, MOD_ROWS, 3 * d), lambda i: (i, 0, 0)),
        out_shape=jax.ShapeDtypeStruct((DEPTH, MOD_ROWS, 3 * d), F32),
        compiler_params=_PARAMS,
        name="modulation",
    )(cvec, w_mod, b_mod.reshape(DEPTH, 1, 3 * d))


def _mla_proj_kernel(ctx_row, x_ref, shift_ref, scale_ref, gpre_ref, win_ref, gq_ref, wuq_ref, gkv_ref,
                     wuk_ref, wuvt_ref, *rest):
    row = pl.program_id(0) if ctx_row is None else ctx_row
    if ctx_row is None:
        tc_ref, ts1_ref, ts2_ref, qn_ref, qr_ref, kn_ref, kr_ref, vt_ref, sg_ref = rest
        tc, ts1, ts2 = tc_ref[...], ts1_ref[...], ts2_ref[...]
        rope = lambda v: _rope(v, tc, ts1, ts2)
    else:
        qn_ref, qr_ref, kn_ref, kr_ref, vt_ref, sg_ref = rest
        rope = lambda v: v
    x = x_ref[0]
    h = _rms(x, gpre_ref[0]) * (1.0 + _mod_row(scale_ref, row)) + _mod_row(shift_ref, row)
    u = _dot(h.astype(BF16), win_ref[0])

    cqn = _rms(u[:, _CQ0:_CKV0], gq_ref[0]).astype(BF16)
    q = _dot(cqn, wuq_ref[0])
    for j in range(_QR_COLS // LANES):
        lo = _QN_COLS + j * LANES
        qr_ref[0, j] = rope(q[:, lo:lo + LANES]).astype(BF16)

    ckvn = _rms(u[:, _CKV0:_KR0], gkv_ref[0]).astype(BF16)
    kn = _dot(ckvn, wuk_ref[0])
    sg = _silu(u[:, _GATE0:])
    for hd in range(N_HEADS):
        hs = slice(hd * NOPE_DIM, (hd + 1) * NOPE_DIM)
        qn_ref[0, hd] = q[:, hs].astype(BF16)
        kn_ref[0, hd] = kn[:, hs].astype(BF16)
        sg_ref[0, hd] = sg[:, hs].astype(BF16)
    vt_ref[0] = _dot_nt(wuvt_ref[0], ckvn).astype(BF16)
    kr_ref[0] = rope(u[:, _KR0:_GATE0]).astype(BF16)


def _mla_proj(x, mod, layer, j, pm, tables, ctx_row, tm):
    b, l, d = x.shape
    assert (ctx_row is None) == (len(tables) == 3)
    tabspec = pl.BlockSpec((tm, LANES), lambda i, t: (t, 0))
    heads = lambda n: pl.BlockSpec((1, n, tm, LANES), lambda i, t: (i, 0, t, 0))
    hshape = lambda n: jax.ShapeDtypeStruct((b, n, l, LANES), BF16)
    nh, npair = N_HEADS, N_HEADS // _HEAD_PAIR
    weights = [pm["w_in"], pm["g_q"], pm["w_uq"], pm["g_kv"], pm["w_uk"], pm["w_uvt"]]
    return pl.pallas_call(
        functools.partial(_mla_proj_kernel, ctx_row),
        grid=(b, l // tm),
        in_specs=[_tok_spec(tm, d), _mod_spec(layer, 0), _mod_spec(layer, 1), _layer_spec(pm["g_pre"], layer)]
                 + [_layer_spec(w, j) for w in weights] + [tabspec] * len(tables),
        out_specs=[heads(nh), heads(npair), heads(nh), _tok_spec(tm, LANES),
                   pl.BlockSpec((1, ATTN_W, tm), lambda i, t: (i, 0, t)), heads(nh)],
        out_shape=[hshape(nh), hshape(npair), hshape(nh), jax.ShapeDtypeStruct((b, l, LANES), BF16),
                   jax.ShapeDtypeStruct((b, ATTN_W, l), BF16), hshape(nh)],
        compiler_params=_PARAMS,
        name="mla_proj",
    )(x, mod, mod, pm["g_pre"], *weights, *tables)


def _attn_kernel(seg_lens, tq, qn_ref, qr_ref, sg_ref, *rest):
    nseg = len(seg_lens)
    seg_refs = rest[:3 * nseg]
    o_ref, k_scr, vt_scr, st_scr, m_scr, p_scr = rest[3 * nseg:]
    lane_half = lax.broadcasted_iota(jnp.int32, (1, LANES), 1) // ROPE_DIM
    hp = qn_ref.shape[1]
    nt = qn_ref.shape[2] // tq
    n_units = hp * nt
    assert n_units >= 2

    for hh in range(hp):
        off = 0
        for s, ls in enumerate(seg_lens):
            kn_ref, kr_ref, vt_ref = seg_refs[3 * s:3 * s + 3]
            k_scr[hh, off:off + ls, :NOPE_DIM] = kn_ref[0, hh]
            in_half = lane_half == hh % _HEAD_PAIR
            k_scr[hh, off:off + ls, NOPE_DIM:] = jnp.where(in_half, kr_ref[0], jnp.zeros((), BF16))
            vt_scr[hh, :V_DIM, off:off + ls] = vt_ref[0, hh * V_DIM:(hh + 1) * V_DIM, :]
            off += ls
        vt_scr[hh, V_DIM:, :] = jnp.ones((_ONES_ROWS, off), BF16)

    def unit(u):
        return u // nt, pl.ds(pl.multiple_of((u % nt) * tq, tq), tq)

    def scores(u):
        hh, rows = unit(u)
        q = jnp.concatenate([qn_ref[0, hh, rows, :], qr_ref[0, hh // _HEAD_PAIR, rows, :]], axis=1)
        st = _dot_nt(k_scr[hh], q)
        st_scr[...] = st
        m_scr[...] = jnp.max(st, axis=0, keepdims=True)

    def probs():
        m = m_scr[...]
        for c in range(0, st_scr.shape[0], _PROB_ROWS):
            rows = slice(c, c + _PROB_ROWS)
            p_scr[rows, :] = jnp.exp2((st_scr[rows, :] - m).astype(BF16))

    def values(u):
        hh, rows = unit(u)
        ot = _dot(vt_scr[hh], p_scr[...])
        ot = ot[:V_DIM] / ot[V_DIM:V_DIM + 1]
        o_ref[0, hh, rows, :] = (ot.T * sg_ref[0, hh, rows, :].astype(F32)).astype(BF16)

    scores(0)
    probs()
    scores(1)

    def iteration(i, carry):
        values(i - 2)
        probs()
        scores(i)
        return carry

    lax.fori_loop(2, n_units, iteration, 0)
    values(n_units - 2)
    probs()
    values(n_units - 1)


def _attention(qn, qr, sg, segments, tq, hp):
    b, nh, lq, _ = qn.shape
    seg_lens = tuple(s[0].shape[2] for s in segments)
    lk = sum(seg_lens)
    assert hp % _HEAD_PAIR == 0 and nh % hp == 0
    pair = lambda l: pl.BlockSpec((1, hp, l, LANES), lambda i, h: (i, h, 0, 0))
    in_specs = [pair(lq), pl.BlockSpec((1, hp // _HEAD_PAIR, lq, LANES), lambda i, h: (i, h, 0, 0)), pair(lq)]
    args = [qn, qr, sg]
    for kn, kr, vt in segments:
        ls = kn.shape[2]
        in_specs += [pair(ls), pl.BlockSpec((1, ls, LANES), lambda i, h: (i, 0, 0)),
                     pl.BlockSpec((1, hp * V_DIM, ls), lambda i, h: (i, h, 0))]
        args += [kn, kr, vt]
    return pl.pallas_call(
        functools.partial(_attn_kernel, seg_lens, tq),
        grid=(b, nh // hp),
        in_specs=in_specs,
        out_specs=pair(lq),
        out_shape=jax.ShapeDtypeStruct((b, nh, lq, LANES), BF16),
        scratch_shapes=[pltpu.VMEM((hp, lk, NOPE_DIM + LANES), BF16),
                        pltpu.VMEM((hp, V_DIM + _ONES_ROWS, lk), BF16),
                        pltpu.VMEM((lk, tq), F32), pltpu.VMEM((1, tq), F32), pltpu.VMEM((lk, tq), BF16)],
        compiler_params=_PARAMS,
        name="attention",
    )(*args)


def _conv_in_math(x, row, rows, shift_ref, scale_ref, gpre_ref, win_ref, u_ref, sg_ref):
    h = _rms(x, gpre_ref[0]) * (1.0 + _mod_row(scale_ref, row)) + _mod_row(shift_ref, row)
    u3 = _dot(h.astype(BF16), win_ref[0])
    u_ref[0, rows, :] = (u3[:, :CONV_W] * jax.nn.sigmoid(u3[:, CONV_W:2 * CONV_W])).astype(BF16)
    sg_ref[0, rows, :] = _silu(u3[:, 2 * CONV_W:]).astype(BF16)


def _row_parts(tm):
    n = _TILE_PARTS if tm % (_TILE_PARTS * LANES) == 0 else 1
    return [slice(i * (tm // n), (i + 1) * (tm // n)) for i in range(n)]


def _out_kernel(ctx_row, fuse_conv_in, og_ref, x_ref, gate_ref, wo_ref, gpost_ref, *rest):
    row = pl.program_id(0) if ctx_row is None else ctx_row
    parts = _row_parts(x_ref.shape[1])
    ys = []
    for rows in parts:
        og = jnp.concatenate([og_ref[0, hd, rows, :] for hd in range(og_ref.shape[1])], axis=1)
        ys.append(_dot(og, wo_ref[0]))
    for rows, y in zip(parts, ys):
        x_new = x_ref[0, rows, :] + _mod_row(gate_ref, row) * _rms(y, gpost_ref[0])
        if fuse_conv_in:
            shift_ref, scale_ref, gpre_ref, win_ref, o_ref, u_ref, sg_ref = rest
            _conv_in_math(x_new, row, rows, shift_ref, scale_ref, gpre_ref, win_ref, u_ref, sg_ref)
        else:
            o_ref, = rest
        o_ref[0, rows, :] = x_new


def _out_residual(og, x, mod, layer, j, pm, ctx_row, tm, next_conv=None):
    b, l, d = x.shape
    in_specs = [pl.BlockSpec((1, og.shape[1], tm, LANES), lambda i, t: (i, 0, t, 0)), _tok_spec(tm, d),
                _mod_spec(layer, 2), _layer_spec(pm["w_o"], j), _layer_spec(pm["g_post"], layer)]
    args = [og, x, mod, pm["w_o"], pm["g_post"]]
    out_specs, out_shape = [_tok_spec(tm, d)], [jax.ShapeDtypeStruct((b, l, d), F32)]
    if next_conv is not None:
        pc, jc = next_conv
        in_specs += [_mod_spec(layer + 1, 0), _mod_spec(layer + 1, 1), _layer_spec(pc["g_pre"], layer + 1),
                     _layer_spec(pc["w_in"], jc)]
        args += [mod, mod, pc["g_pre"], pc["w_in"]]
        out_specs += [_tok_spec(tm, CONV_W)] * 2
        out_shape += [jax.ShapeDtypeStruct((b, l, CONV_W), BF16)] * 2
    res = pl.pallas_call(
        functools.partial(_out_kernel, ctx_row, next_conv is not None),
        grid=(b, l // tm),
        in_specs=in_specs,
        out_specs=out_specs,
        out_shape=out_shape,
        compiler_params=_PARAMS,
        name="out_residual",
    )(*args)
    return res if next_conv is not None else res[0]


def _conv_in_kernel(ctx_row, x_ref, shift_ref, scale_ref, gpre_ref, win_ref, u_ref, sg_ref):
    row = pl.program_id(0) if ctx_row is None else ctx_row
    for rows in _row_parts(x_ref.shape[1]):
        _conv_in_math(x_ref[0, rows, :], row, rows, shift_ref, scale_ref, gpre_ref, win_ref, u_ref, sg_ref)


def _conv_in(x, mod, layer, j, pc, ctx_row, tm):
    b, l, d = x.shape
    return pl.pallas_call(
        functools.partial(_conv_in_kernel, ctx_row),
        grid=(b, l // tm),
        in_specs=[_tok_spec(tm, d), _mod_spec(layer, 0), _mod_spec(layer, 1), _layer_spec(pc["g_pre"], layer),
                  _layer_spec(pc["w_in"], j)],
        out_specs=[_tok_spec(tm, CONV_W)] * 2,
        out_shape=[jax.ShapeDtypeStruct((b, l, CONV_W), BF16)] * 2,
        compiler_params=_PARAMS,
        name="conv_in",
    )(x, mod, mod, pc["g_pre"], pc["w_in"])


def _conv_out_kernel(ctx_row, tm, u_ref, sg_ref, x_ref, gate_ref, wdw_ref, bdw_ref, gln_ref, bln_ref,
                     wo_ref, gpost_ref, o_ref, win, cv):
    row = pl.program_id(0) if ctx_row is None else ctx_row
    t = pl.program_id(1)
    nt = pl.num_programs(1)
    l = u_ref.shape[1]
    r0 = pl.multiple_of(t * tm, tm)

    top0 = pl.multiple_of(jnp.maximum(r0 - _HALO, 0), _HALO)
    bot0 = pl.multiple_of(jnp.minimum(r0 + tm, l - _HALO), _HALO)
    top = u_ref[0, pl.ds(top0, _HALO), :].astype(F32) * (t > 0).astype(F32)
    mid = u_ref[0, pl.ds(r0, tm), :].astype(F32)
    bot = u_ref[0, pl.ds(bot0, _HALO), :].astype(F32) * (t < nt - 1).astype(F32)
    for lc in range(CONV_W // LANES):
        ls = slice(lc * LANES, (lc + 1) * LANES)
        win[lc, 0:_HALO] = top[:, ls]
        win[lc, _HALO:_HALO + tm] = mid[:, ls]
        win[lc, _HALO + tm:] = bot[:, ls]

    for rc in range(tm // _CONV_ROWS):
        for lc in range(CONV_W // LANES):
            ls = slice(lc * LANES, (lc + 1) * LANES)
            acc = jnp.broadcast_to(bdw_ref[0, :, ls], (_CONV_ROWS, LANES))
            for k in range(CONV_K):
                lo = rc * _CONV_ROWS + k + _HALO - CONV_PAD
                acc = acc + win[lc, lo:lo + _CONV_ROWS, :] * wdw_ref[0, k:k + 1, ls]
            cv[rc * _CONV_ROWS:(rc + 1) * _CONV_ROWS, ls] = acc

    c = cv[...]
    mu = jnp.mean(c, axis=-1, keepdims=True)
    cc = c - mu
    var = jnp.mean(cc * cc, axis=-1, keepdims=True)
    ln = cc * lax.rsqrt(var + EPS) * gln_ref[0] + bln_ref[0]
    z = (_silu(ln) * sg_ref[0].astype(F32)).astype(BF16)
    y = _dot(z, wo_ref[0])
    o_ref[0] = x_ref[0] + _mod_row(gate_ref, row) * _rms(y, gpost_ref[0])


def _conv_out(u, sg, x, mod, layer, j, pc, ctx_row, tm):
    b, l, d = x.shape
    per_conv = [pc["w_dw"], pc["b_dw"], pc["g_ln"], pc["b_ln"], pc["w_o"]]
    return pl.pallas_call(
        functools.partial(_conv_out_kernel, ctx_row, tm),
        grid=(b, l // tm),
        in_specs=[pl.BlockSpec((1, l, CONV_W), lambda i, t: (i, 0, 0)), _tok_spec(tm, CONV_W), _tok_spec(tm, d),
                  _mod_spec(layer, 2)] + [_layer_spec(w, j) for w in per_conv] + [_layer_spec(pc["g_post"], layer)],
        out_specs=_tok_spec(tm, d),
        out_shape=jax.ShapeDtypeStruct((b, l, d), F32),
        scratch_shapes=[pltpu.VMEM((CONV_W // LANES, tm + 2 * _HALO, LANES), F32), pltpu.VMEM((tm, CONV_W), F32)],
        compiler_params=_PARAMS,
        name="conv_out",
    )(u, sg, x, mod, *per_conv, pc["g_post"])


def _rope_tables(n_tokens):
    f32 = np.float32
    rows_n = n_tokens // GRID_W
    rows = np.repeat(np.arange(rows_n, dtype=f32), GRID_W)
    cols = np.tile(np.arange(GRID_W, dtype=f32), rows_n)
    half = ROPE_DIM // 2
    freqs = (f32(1.0) / (f32(ROPE_BASE) ** (np.arange(0, half, 2, dtype=f32) / f32(half)))).astype(f32)
    ang_r = rows[:, None] * freqs[None, :]
    ang_c = cols[:, None] * freqs[None, :]
    zero = np.zeros_like(ang_r)
    cos64 = np.concatenate([np.cos(ang_r), np.cos(ang_r), np.cos(ang_c), np.cos(ang_c)], axis=-1)
    s1_64 = np.concatenate([zero, np.sin(ang_r), zero, np.sin(ang_c)], axis=-1)
    s2_64 = np.concatenate([-np.sin(ang_r), zero, -np.sin(ang_c), zero], axis=-1)
    dup = lambda a: jnp.asarray(np.concatenate([a, a], axis=-1).astype(f32))
    return dup(cos64), dup(s1_64), dup(s2_64)


def _mla_params(g_pre, g_post, mla_w_in, mla_g_q, mla_w_uq, mla_g_kv, mla_w_ukv, mla_w_o):
    n = mla_w_in.shape[0]
    kr = mla_w_in[:, :, _KR0:_KR0 + ROPE_DIM]
    w_in = jnp.concatenate([mla_w_in[:, :, :_KR0], kr, kr, mla_w_in[:, :, _KR0 + ROPE_DIM:]], axis=2).astype(BF16)
    w_uq = mla_w_uq.reshape(n, Q_LORA, N_HEADS, NOPE_DIM + ROPE_DIM)
    w_uq = jnp.concatenate([w_uq[..., :NOPE_DIM].reshape(n, Q_LORA, _QN_COLS),
                            w_uq[..., NOPE_DIM:].reshape(n, Q_LORA, _QR_COLS)], axis=2).astype(BF16)
    w_ukv = mla_w_ukv.reshape(n, KV_LORA, N_HEADS, NOPE_DIM + V_DIM)
    w_uk = w_ukv[..., :NOPE_DIM].reshape(n, KV_LORA, _QN_COLS).astype(BF16)
    w_uvt = jnp.swapaxes(w_ukv[..., NOPE_DIM:].reshape(n, KV_LORA, ATTN_W), 1, 2).astype(BF16)
    scale = math.log2(math.e) / math.sqrt(NOPE_DIM + ROPE_DIM)
    return dict(g_pre=g_pre, g_post=g_post, w_in=w_in, g_q=(mla_g_q * scale)[:, None, :], w_uq=w_uq,
                g_kv=mla_g_kv[:, None, :], w_uk=w_uk, w_uvt=w_uvt, w_o=mla_w_o.astype(BF16))


def kernel(x, c, ctx, c_ctx, w_mod, b_mod, g_pre, g_post, mla_w_in, mla_g_q, mla_w_uq, mla_g_kv, mla_w_ukv,
           mla_w_o, cv_w_in, cv_w_dw, cv_b_dw, cv_g_ln, cv_b_ln, cv_w_o):
    b, s, d = x.shape
    lc = ctx.shape[1]
    ctx_row = b
    assert b < MOD_ROWS and s % GRID_W == 0

    cvec = jnp.zeros((MOD_ROWS, d), F32).at[:b].set(c).at[ctx_row].set(c_ctx)
    mod = _modulation(cvec, w_mod, b_mod)

    tables_x = _rope_tables(s)
    g_pre3, g_post3 = g_pre[:, None, :], g_post[:, None, :]
    pm = _mla_params(g_pre3, g_post3, mla_w_in, mla_g_q, mla_w_uq, mla_g_kv, mla_w_ukv, mla_w_o)
    pc = dict(g_pre=g_pre3, g_post=g_post3, w_in=cv_w_in.astype(BF16), w_dw=cv_w_dw.reshape(-1, CONV_K, CONV_W),
              b_dw=cv_b_dw[:, None, :], g_ln=cv_g_ln[:, None, :], b_ln=cv_b_ln[:, None, :], w_o=cv_w_o.astype(BF16))

    conv_inputs = None
    for i in range(DEPTH):
        last = i == DEPTH - 1
        j = i // N_MIXERS
        if i % N_MIXERS == 0:
            next_conv = (pc, (i + 1) // N_MIXERS) if not last and (i + 1) % N_MIXERS == 1 else None
            qn, qr, kn, kr, vt, sg = _mla_proj(x, mod, i, j, pm, tables_x, None, _TM_PROJ)
            qn_c, qr_c, kn_c, kr_c, vt_c, sg_c = _mla_proj(ctx, mod, i, j, pm, (), ctx_row, lc)
            og = _attention(qn, qr, sg, [(kn, kr, vt), (kn_c, kr_c, vt_c)], _TQ, _HEADS_X)
            res_x = _out_residual(og, x, mod, i, j, pm, None, _TM_PROJ, next_conv)
            res_c = None
            if not last:
                og_c = _attention(qn_c, qr_c, sg_c, [(kn_c, kr_c, vt_c)], lc, N_HEADS)
                res_c = _out_residual(og_c, ctx, mod, i, j, pm, ctx_row, lc, next_conv)
            if next_conv is None:
                x, ctx = res_x, (ctx if res_c is None else res_c)
            else:
                x, ctx = res_x[0], res_c[0]
                conv_inputs = (res_x[1:], res_c[1:])
        else:
            if conv_inputs is None:
                conv_inputs = (_conv_in(x, mod, i, j, pc, None, _TM_PROJ),
                               _conv_in(ctx, mod, i, j, pc, ctx_row, lc) if not last else None)
            (u, sg), uc_sgc = conv_inputs
            conv_inputs = None
            x = _conv_out(u, sg, x, mod, i, j, pc, None, _TM_CONV)
            if not last:
                ctx = _conv_out(uc_sgc[0], uc_sgc[1], ctx, mod, i, j, pc, ctx_row, lc)
    return x
```

```python
import functools
import math

import jax
import jax.numpy as jnp
import numpy as np
from jax import lax
from jax.experimental import pallas as pl
from jax.experimental.pallas import tpu as pltpu

D_MODEL = 1024
DEPTH = 4
GRID_W = 64
N_MIXERS = 2
N_HEADS = 8
NOPE_DIM = 128
ROPE_DIM = 64
V_DIM = 128
Q_LORA = 384
KV_LORA = 256
ATTN_W = N_HEADS * V_DIM
ROPE_BASE = 10000.0
CONV_W = D_MODEL
CONV_K = 31
CONV_PAD = CONV_K // 2
EPS = 1e-6

LANES = 128
MOD_ROWS = 16
VMEM_LIMIT = 56 * 1024 * 1024
_HEAD_PAIR = 2
_HEADS_X = 2
_ONES_ROWS = 16
_PROB_ROWS = 64
_TM_PROJ = 512
_TM_CONV = 1024
_TILE_PARTS = 2
_TQ = 1024
_HALO = 16
_CONV_ROWS = 128
assert _HALO >= CONV_PAD + 1

F32 = jnp.float32
BF16 = jnp.bfloat16

_CQ0, _CKV0, _KR0, _GATE0 = 0, Q_LORA, Q_LORA + KV_LORA, Q_LORA + KV_LORA + LANES
_WIN_COLS = _GATE0 + ATTN_W
_QN_COLS = N_HEADS * NOPE_DIM
_QR_COLS = N_HEADS * ROPE_DIM


def _silu(v):
    return v * jax.nn.sigmoid(v)


def _rms(v, g):
    return v * lax.rsqrt(jnp.mean(v * v, axis=-1, keepdims=True) + EPS) * g


def _dot(a, b):
    return jnp.dot(a, b, preferred_element_type=F32)


def _dot_nt(a, b):
    return lax.dot_general(a, b, (((1,), (1,)), ((), ())), preferred_element_type=F32)


def _rope(v, c, s1, s2):
    return v * c + pltpu.roll(v, 16, axis=1) * s1 + pltpu.roll(v, LANES - 16, axis=1) * s2


def _mod_row(ref, row):
    return ref[0, pl.ds(row, 1), :]


def _layer_spec(arr, idx):
    zeros = (0,) * (arr.ndim - 1)
    return pl.BlockSpec((1,) + arr.shape[1:], lambda i, t: (idx,) + zeros)


def _mod_spec(layer, part):
    return pl.BlockSpec((1, MOD_ROWS, D_MODEL), lambda i, t: (layer, 0, part))


def _tok_spec(tm, width):
    return pl.BlockSpec((1, tm, width), lambda i, t: (i, t, 0))


_PARAMS = pltpu.CompilerParams(vmem_limit_bytes=VMEM_LIMIT)


def _mod_kernel(c_ref, w_ref, b_ref, o_ref):
    sc = _silu(c_ref[...]).astype(BF16)
    o_ref[0] = _dot(sc, w_ref[0].astype(BF16)) + b_ref[0]


def _modulation(cvec, w_mod, b_mod):
    d = D_MODEL
    return pl.pallas_call(
        _mod_kernel,
        grid=(DEPTH,),
        in_specs=[
            pl.BlockSpec((MOD_ROWS, d), lambda i: (0, 0)),
            pl.BlockSpec((1, d, 3 * d), lambda i: (i, 0, 0)),
            pl.BlockSpec((1, 1, 3 * d), lambda i: (i, 0, 0)),
        ],
        out_specs=pl.BlockSpec((1, MOD_ROWS, 3 * d), lambda i: (i, 0, 0)),
        out_shape=jax.ShapeDtypeStruct((DEPTH, MOD_ROWS, 3 * d), F32),
        compiler_params=_PARAMS,
        name="modulation",
    )(cvec, w_mod, b_mod.reshape(DEPTH, 1, 3 * d))


def _mla_proj_kernel(ctx_row, x_ref, shift_ref, scale_ref, gpre_ref, win_ref, gq_ref, wuq_ref, gkv_ref,
                     wuk_ref, wuvt_ref, *rest):
    row = pl.program_id(0) if ctx_row is None else ctx_row
    if ctx_row is None:
        tc_ref, ts1_ref, ts2_ref, qn_ref, qr_ref, kn_ref, kr_ref, vt_ref, sg_ref = rest
        tc, ts1, ts2 = tc_ref[...], ts1_ref[...], ts2_ref[...]
        rope = lambda v: _rope(v, tc, ts1, ts2)
    else:
        qn_ref, qr_ref, kn_ref, kr_ref, vt_ref, sg_ref = rest
        rope = lambda v: v
    x = x_ref[0]
    h = _rms(x, gpre_ref[0]) * (1.0 + _mod_row(scale_ref, row)) + _mod_row(shift_ref, row)
    u = _dot(h.astype(BF16), win_ref[0])

    cqn = _rms(u[:, _CQ0:_CKV0], gq_ref[0]).astype(BF16)
    q = _dot(cqn, wuq_ref[0])
    for j in range(_QR_COLS // LANES):
        lo = _QN_COLS + j * LANES
        qr_ref[0, j] = rope(q[:, lo:lo + LANES]).astype(BF16)

    ckvn = _rms(u[:, _CKV0:_KR0], gkv_ref[0]).astype(BF16)
    kn = _dot(ckvn, wuk_ref[0])
    sg = _silu(u[:, _GATE0:])
    for hd in range(N_HEADS):
        hs = slice(hd * NOPE_DIM, (hd + 1) * NOPE_DIM)
        qn_ref[0, hd] = q[:, hs].astype(BF16)
        kn_ref[0, hd] = kn[:, hs].astype(BF16)
        sg_ref[0, hd] = sg[:, hs].astype(BF16)
    vt_ref[0] = _dot_nt(wuvt_ref[0], ckvn).astype(BF16)
    kr_ref[0] = rope(u[:, _KR0:_GATE0]).astype(BF16)


def _mla_proj(x, mod, layer, j, pm, tables, ctx_row, tm):
    b, l, d = x.shape
    assert (ctx_row is None) == (len(tables) == 3)
    tabspec = pl.BlockSpec((tm, LANES), lambda i, t: (t, 0))
    heads = lambda n: pl.BlockSpec((1, n, tm, LANES), lambda i, t: (i, 0, t, 0))
    hshape = lambda n: jax.ShapeDtypeStruct((b, n, l, LANES), BF16)
    nh, npair = N_HEADS, N_HEADS // _HEAD_PAIR
    weights = [pm["w_in"], pm["g_q"], pm["w_uq"], pm["g_kv"], pm["w_uk"], pm["w_uvt"]]
    return pl.pallas_call(
        functools.partial(_mla_proj_kernel, ctx_row),
        grid=(b, l // tm),
        in_specs=[_tok_spec(tm, d), _mod_spec(layer, 0), _mod_spec(layer, 1), _layer_spec(pm["g_pre"], layer)]
                 + [_layer_spec(w, j) for w in weights] + [tabspec] * len(tables),
        out_specs=[heads(nh), heads(npair), heads(nh), _tok_spec(tm, LANES),
                   pl.BlockSpec((1, ATTN_W, tm), lambda i, t: (i, 0, t)), heads(nh)],
        out_shape=[hshape(nh), hshape(npair), hshape(nh), jax.ShapeDtypeStruct((b, l, LANES), BF16),
                   jax.ShapeDtypeStruct((b, ATTN_W, l), BF16), hshape(nh)],
        compiler_params=_PARAMS,
        name="mla_proj",
    )(x, mod, mod, pm["g_pre"], *weights, *tables)


def _attn_kernel(seg_lens, tq, qn_ref, qr_ref, sg_ref, *rest):
    nseg = len(seg_lens)
    seg_refs = rest[:3 * nseg]
    o_ref, k_scr, vt_scr, st_scr, m_scr, p_scr = rest[3 * nseg:]
    lane_half = lax.broadcasted_iota(jnp.int32, (1, LANES), 1) // ROPE_DIM
    hp = qn_ref.shape[1]
    nt = qn_ref.shape[2] // tq
    n_units = hp * nt
    assert n_units >= 2

    for hh in range(hp):
        off = 0
        for s, ls in enumerate(seg_lens):
            kn_ref, kr_ref, vt_ref = seg_refs[3 * s:3 * s + 3]
            k_scr[hh, off:off + ls, :NOPE_DIM] = kn_ref[0, hh]
            in_half = lane_half == hh % _HEAD_PAIR
            k_scr[hh, off:off + ls, NOPE_DIM:] = jnp.where(in_half, kr_ref[0], jnp.zeros((), BF16))
            vt_scr[hh, :V_DIM, off:off + ls] = vt_ref[0, hh * V_DIM:(hh + 1) * V_DIM, :]
            off += ls
        vt_scr[hh, V_DIM:, :] = jnp.ones((_ONES_ROWS, off), BF16)

    def unit(u):
        return u // nt, pl.ds(pl.multiple_of((u % nt) * tq, tq), tq)

    def scores(u):
        hh, rows = unit(u)
        q = jnp.concatenate([qn_ref[0, hh, rows, :], qr_ref[0, hh // _HEAD_PAIR, rows, :]], axis=1)
        st = _dot_nt(k_scr[hh], q)
        st_scr[...] = st
        m_scr[...] = jnp.max(st, axis=0, keepdims=True)

    def probs():
        m = m_scr[...]
        for c in range(0, st_scr.shape[0], _PROB_ROWS):
            rows = slice(c, c + _PROB_ROWS)
            p_scr[rows, :] = jnp.exp2((st_scr[rows, :] - m).astype(BF16))

    def values(u):
        hh, rows = unit(u)
        ot = _dot(vt_scr[hh], p_scr[...])
        ot = ot[:V_DIM] / ot[V_DIM:V_DIM + 1]
        o_ref[0, hh, rows, :] = (ot.T * sg_ref[0, hh, rows, :].astype(F32)).astype(BF16)

    scores(0)
    probs()
    scores(1)

    def iteration(i, carry):
        values(i - 2)
        probs()
        scores(i)
        return carry

    lax.fori_loop(2, n_units, iteration, 0)
    values(n_units - 2)
    probs()
    values(n_units - 1)


def _attention(qn, qr, sg, segments, tq, hp):
    b, nh, lq, _ = qn.shape
    seg_lens = tuple(s[0].shape[2] for s in segments)
    lk = sum(seg_lens)
    assert hp % _HEAD_PAIR == 0 and nh % hp == 0
    pair = lambda l: pl.BlockSpec((1, hp, l, LANES), lambda i, h: (i, h, 0, 0))
    in_specs = [pair(lq), pl.BlockSpec((1, hp // _HEAD_PAIR, lq, LANES), lambda i, h: (i, h, 0, 0)), pair(lq)]
    args = [qn, qr, sg]
    for kn, kr, vt in segments:
        ls = kn.shape[2]
        in_specs += [pair(ls), pl.BlockSpec((1, ls, LANES), lambda i, h: (i, 0, 0)),
                     pl.BlockSpec((1, hp * V_DIM, ls), lambda i, h: (i, h, 0))]
        args += [kn, kr, vt]
    return pl.pallas_call(
        functools.partial(_attn_kernel, seg_lens, tq),
        grid=(b, nh // hp),
        in_specs=in_specs,
        out_specs=pair(lq),
        out_shape=jax.ShapeDtypeStruct((b, nh, lq, LANES), BF16),
        scratch_shapes=[pltpu.VMEM((hp, lk, NOPE_DIM + LANES), BF16),
                        pltpu.VMEM((hp, V_DIM + _ONES_ROWS, lk), BF16),
                        pltpu.VMEM((lk, tq), F32), pltpu.VMEM((1, tq), F32), pltpu.VMEM((lk, tq), BF16)],
        compiler_params=_PARAMS,
        name="attention",
    )(*args)


def _conv_in_math(x, row, rows, shift_ref, scale_ref, gpre_ref, win_ref, u_ref, sg_ref):
    h = _rms(x, gpre_ref[0]) * (1.0 + _mod_row(scale_ref, row)) + _mod_row(shift_ref, row)
    u3 = _dot(h.astype(BF16), win_ref[0])
    u_ref[0, rows, :] = (u3[:, :CONV_W] * jax.nn.sigmoid(u3[:, CONV_W:2 * CONV_W])).astype(BF16)
    sg_ref[0, rows, :] = _silu(u3[:, 2 * CONV_W:]).astype(BF16)


def _row_parts(tm):
    n = _TILE_PARTS if tm % (_TILE_PARTS * LANES) == 0 else 1
    return [slice(i * (tm // n), (i + 1) * (tm // n)) for i in range(n)]


def _out_kernel(ctx_row, fuse_conv_in, og_ref, x_ref, gate_ref, wo_ref, gpost_ref, *rest):
    row = pl.program_id(0) if ctx_row is None else ctx_row
    parts = _row_parts(x_ref.shape[1])
    ys = []
    for rows in parts:
        og = jnp.concatenate([og_ref[0, hd, rows, :] for hd in range(og_ref.shape[1])], axis=1)
        ys.append(_dot(og, wo_ref[0]))
    for rows, y in zip(parts, ys):
        x_new = x_ref[0, rows, :] + _mod_row(gate_ref, row) * _rms(y, gpost_ref[0])
        if fuse_conv_in:
            shift_ref, scale_ref, gpre_ref, win_ref, o_ref, u_ref, sg_ref = rest
            _conv_in_math(x_new, row, rows, shift_ref, scale_ref, gpre_ref, win_ref, u_ref, sg_ref)
        else:
            o_ref, = rest
        o_ref[0, rows, :] = x_new


def _out_residual(og, x, mod, layer, j, pm, ctx_row, tm, next_conv=None):
    b, l, d = x.shape
    in_specs = [pl.BlockSpec((1, og.shape[1], tm, LANES), lambda i, t: (i, 0, t, 0)), _tok_spec(tm, d),
                _mod_spec(layer, 2), _layer_spec(pm["w_o"], j), _layer_spec(pm["g_post"], layer)]
    args = [og, x, mod, pm["w_o"], pm["g_post"]]
    out_specs, out_shape = [_tok_spec(tm, d)], [jax.ShapeDtypeStruct((b, l, d), F32)]
    if next_conv is not None:
        pc, jc = next_conv
        in_specs += [_mod_spec(layer + 1, 0), _mod_spec(layer + 1, 1), _layer_spec(pc["g_pre"], layer + 1),
                     _layer_spec(pc["w_in"], jc)]
        args += [mod, mod, pc["g_pre"], pc["w_in"]]
        out_specs += [_tok_spec(tm, CONV_W)] * 2
        out_shape += [jax.ShapeDtypeStruct((b, l, CONV_W), BF16)] * 2
    res = pl.pallas_call(
        functools.partial(_out_kernel, ctx_row, next_conv is not None),
        grid=(b, l // tm),
        in_specs=in_specs,
        out_specs=out_specs,
        out_shape=out_shape,
        compiler_params=_PARAMS,
        name="out_residual",
    )(*args)
    return res if next_conv is not None else res[0]


def _conv_in_kernel(ctx_row, x_ref, shift_ref, scale_ref, gpre_ref, win_ref, u_ref, sg_ref):
    row = pl.program_id(0) if ctx_row is None else ctx_row
    for rows in _row_parts(x_ref.shape[1]):
        _conv_in_math(x_ref[0, rows, :], row, rows, shift_ref, scale_ref, gpre_ref, win_ref, u_ref, sg_ref)


def _conv_in(x, mod, layer, j, pc, ctx_row, tm):
    b, l, d = x.shape
    return pl.pallas_call(
        functools.partial(_conv_in_kernel, ctx_row),
        grid=(b, l // tm),
        in_specs=[_tok_spec(tm, d), _mod_spec(layer, 0), _mod_spec(layer, 1), _layer_spec(pc["g_pre"], layer),
                  _layer_spec(pc["w_in"], j)],
        out_specs=[_tok_spec(tm, CONV_W)] * 2,
        out_shape=[jax.ShapeDtypeStruct((b, l, CONV_W), BF16)] * 2,
        compiler_params=_PARAMS,
        name="conv_in",
    )(x, mod, mod, pc["g_pre"], pc["w_in"])


def _conv_out_kernel(ctx_row, tm, u_ref, sg_ref, x_ref, gate_ref, wdw_ref, bdw_ref, gln_ref, bln_ref,
                     wo_ref, gpost_ref, o_ref, win, cv):
    row = pl.program_id(0) if ctx_row is None else ctx_row
    t = pl.program_id(1)
    nt = pl.num_programs(1)
    l = u_ref.shape[1]
    r0 = pl.multiple_of(t * tm, tm)

    top0 = pl.multiple_of(jnp.maximum(r0 - _HALO, 0), _HALO)
    bot0 = pl.multiple_of(jnp.minimum(r0 + tm, l - _HALO), _HALO)
    top = u_ref[0, pl.ds(top0, _HALO), :].astype(F32) * (t > 0).astype(F32)
    mid = u_ref[0, pl.ds(r0, tm), :].astype(F32)
    bot = u_ref[0, pl.ds(bot0, _HALO), :].astype(F32) * (t < nt - 1).astype(F32)
    for lc in range(CONV_W // LANES):
        ls = slice(lc * LANES, (lc + 1) * LANES)
        win[lc, 0:_HALO] = top[:, ls]
        win[lc, _HALO:_HALO + tm] = mid[:, ls]
        win[lc, _HALO + tm:] = bot[:, ls]

    for rc in range(tm // _CONV_ROWS):
        for lc in range(CONV_W // LANES):
            ls = slice(lc * LANES, (lc + 1) * LANES)
            acc = jnp.broadcast_to(bdw_ref[0, :, ls], (_CONV_ROWS, LANES))
            for k in range(CONV_K):
                lo = rc * _CONV_ROWS + k + _HALO - CONV_PAD
                acc = acc + win[lc, lo:lo + _CONV_ROWS, :] * wdw_ref[0, k:k + 1, ls]
            cv[rc * _CONV_ROWS:(rc + 1) * _CONV_ROWS, ls] = acc

    c = cv[...]
    mu = jnp.mean(c, axis=-1, keepdims=True)
    cc = c - mu
    var = jnp.mean(cc * cc, axis=-1, keepdims=True)
    ln = cc * lax.rsqrt(var + EPS) * gln_ref[0] + bln_ref[0]
    z = (_silu(ln) * sg_ref[0].astype(F32)).astype(BF16)
    y = _dot(z, wo_ref[0])
    o_ref[0] = x_ref[0] + _mod_row(gate_ref, row) * _rms(y, gpost_ref[0])


def _conv_out(u, sg, x, mod, layer, j, pc, ctx_row, tm):
    b, l, d = x.shape
    per_conv = [pc["w_dw"], pc["b_dw"], pc["g_ln"], pc["b_ln"], pc["w_o"]]
    return pl.pallas_call(
        functools.partial(_conv_out_kernel, ctx_row, tm),
        grid=(b, l // tm),
        in_specs=[pl.BlockSpec((1, l, CONV_W), lambda i, t: (i, 0, 0)), _tok_spec(tm, CONV_W), _tok_spec(tm, d),
                  _mod_spec(layer, 2)] + [_layer_spec(w, j) for w in per_conv] + [_layer_spec(pc["g_post"], layer)],
        out_specs=_tok_spec(tm, d),
        out_shape=jax.ShapeDtypeStruct((b, l, d), F32),
        scratch_shapes=[pltpu.VMEM((CONV_W // LANES, tm + 2 * _HALO, LANES), F32), pltpu.VMEM((tm, CONV_W), F32)],
        compiler_params=_PARAMS,
        name="conv_out",
    )(u, sg, x, mod, *per_conv, pc["g_post"])


def _rope_tables(n_tokens):
    f32 = np.float32
    rows_n = n_tokens // GRID_W
    rows = np.repeat(np.arange(rows_n, dtype=f32), GRID_W)
    cols = np.tile(np.arange(GRID_W, dtype=f32), rows_n)
    half = ROPE_DIM // 2
    freqs = (f32(1.0) / (f32(ROPE_BASE) ** (np.arange(0, half, 2, dtype=f32) / f32(half)))).astype(f32)
    ang_r = rows[:, None] * freqs[None, :]
    ang_c = cols[:, None] * freqs[None, :]
    zero = np.zeros_like(ang_r)
    cos64 = np.concatenate([np.cos(ang_r), np.cos(ang_r), np.cos(ang_c), np.cos(ang_c)], axis=-1)
    s1_64 = np.concatenate([zero, np.sin(ang_r), zero, np.sin(ang_c)], axis=-1)
    s2_64 = np.concatenate([-np.sin(ang_r), zero, -np.sin(ang_c), zero], axis=-1)
    dup = lambda a: jnp.asarray(np.concatenate([a, a], axis=-1).astype(f32))
    return dup(cos64), dup(s1_64), dup(s2_64)


def _mla_params(g_pre, g_post, mla_w_in, mla_g_q, mla_w_uq, mla_g_kv, mla_w_ukv, mla_w_o):
    n = mla_w_in.shape[0]
    kr = mla_w_in[:, :, _KR0:_KR0 + ROPE_DIM]
    w_in = jnp.concatenate([mla_w_in[:, :, :_KR0], kr, kr, mla_w_in[:, :, _KR0 + ROPE_DIM:]], axis=2).astype(BF16)
    w_uq = mla_w_uq.reshape(n, Q_LORA, N_HEADS, NOPE_DIM + ROPE_DIM)
    w_uq = jnp.concatenate([w_uq[..., :NOPE_DIM].reshape(n, Q_LORA, _QN_COLS),
                            w_uq[..., NOPE_DIM:].reshape(n, Q_LORA, _QR_COLS)], axis=2).astype(BF16)
    w_ukv = mla_w_ukv.reshape(n, KV_LORA, N_HEADS, NOPE_DIM + V_DIM)
    w_uk = w_ukv[..., :NOPE_DIM].reshape(n, KV_LORA, _QN_COLS).astype(BF16)
    w_uvt = jnp.swapaxes(w_ukv[..., NOPE_DIM:].reshape(n, KV_LORA, ATTN_W), 1, 2).astype(BF16)
    scale = math.log2(math.e) / math.sqrt(NOPE_DIM + ROPE_DIM)
    return dict(g_pre=g_pre, g_post=g_post, w_in=w_in, g_q=(mla_g_q * scale)[:, None, :], w_uq=w_uq,
                g_kv=mla_g_kv[:, None, :], w_uk=w_uk, w_uvt=w_uvt, w_o=mla_w_o.astype(BF16))


def kernel(x, c, ctx, c_ctx, w_mod, b_mod, g_pre, g_post, mla_w_in, mla_g_q, mla_w_uq, mla_g_kv, mla_w_ukv,
           mla_w_o, cv_w_in, cv_w_dw, cv_b_dw, cv_g_ln, cv_b_ln, cv_w_o):
    b, s, d = x.shape
    lc = ctx.shape[1]
    ctx_row = b
    assert b < MOD_ROWS and s % GRID_W == 0

    cvec = jnp.zeros((MOD_ROWS, d), F32).at[:b].set(c).at[ctx_row].set(c_ctx)
    mod = _modulation(cvec, w_mod, b_mod)

    tables_x = _rope_tables(s)
    g_pre3, g_post3 = g_pre[:, None, :], g_post[:, None, :]
    pm = _mla_params(g_pre3, g_post3, mla_w_in, mla_g_q, mla_w_uq, mla_g_kv, mla_w_ukv, mla_w_o)
    pc = dict(g_pre=g_pre3, g_post=g_post3, w_in=cv_w_in.astype(BF16), w_dw=cv_w_dw.reshape(-1, CONV_K, CONV_W),
              b_dw=cv_b_dw[:, None, :], g_ln=cv_g_ln[:, None, :], b_ln=cv_b_ln[:, None, :], w_o=cv_w_o.astype(BF16))

    conv_inputs = None
    for i in range(DEPTH):
        last = i == DEPTH - 1
        j = i // N_MIXERS
        if i % N_MIXERS == 0:
            next_conv = (pc, (i + 1) // N_MIXERS) if not last and (i + 1) % N_MIXERS == 1 else None
            qn, qr, kn, kr, vt, sg = _mla_proj(x, mod, i, j, pm, tables_x, None, _TM_PROJ)
            qn_c, qr_c, kn_c, kr_c, vt_c, sg_c = _mla_proj(ctx, mod, i, j, pm, (), ctx_row, lc)
            og = _attention(qn, qr, sg, [(kn, kr, vt), (kn_c, kr_c, vt_c)], _TQ, _HEADS_X)
            res_x = _out_residual(og, x, mod, i, j, pm, None, _TM_PROJ, next_conv)
            res_c = None
            if not last:
                og_c = _attention(qn_c, qr_c, sg_c, [(kn_c, kr_c, vt_c)], lc, N_HEADS)
                res_c = _out_residual(og_c, ctx, mod, i, j, pm, ctx_row, lc, next_conv)
            if next_conv is None:
                x, ctx = res_x, (ctx if res_c is None else res_c)
            else:
                x, ctx = res_x[0], res_c[0]
                conv_inputs = (res_x[1:], res_c[1:])
        else:
            if conv_inputs is None:
                conv_inputs = (_conv_in(x, mod, i, j, pc, None, _TM_PROJ),
                               _conv_in(ctx, mod, i, j, pc, ctx_row, lc) if not last else None)
            (u, sg), uc_sgc = conv_inputs
            conv_inputs = None
            x = _conv_out(u, sg, x, mod, i, j, pc, None, _TM_CONV)
            if not last:
                ctx = _conv_out(uc_sgc[0], uc_sgc[1], ctx, mod, i, j, pc, ctx_row, lc)
    return x
```

```python
import functools
import math

import jax
import jax.numpy as jnp
import numpy as np
from jax import lax
from jax.experimental import pallas as pl
from jax.experimental.pallas import tpu as pltpu

D_MODEL = 1024
DEPTH = 4
GRID_W = 64
N_MIXERS = 2
N_HEADS = 8
NOPE_DIM = 128
ROPE_DIM = 64
V_DIM = 128
Q_LORA = 384
KV_LORA = 256
ATTN_W = N_HEADS * V_DIM
ROPE_BASE = 10000.0
CONV_W = D_MODEL
CONV_K = 31
CONV_PAD = CONV_K // 2
EPS = 1e-6

LANES = 128
MOD_ROWS = 16
VMEM_LIMIT = 56 * 1024 * 1024
_HEAD_PAIR = 2
_HEADS_X = 2
_ONES_ROWS = 16
_PROB_ROWS = 64
_TM_MLA = 512
_TM_PROJ = 1024
_TM_CONV = 1024
_TILE_PARTS = 4
_TQ = 1024
_HALO = 16
_CONV_ROWS = 128
assert _HALO >= CONV_PAD + 1

F32 = jnp.float32
BF16 = jnp.bfloat16

_CQ0, _CKV0, _KR0, _GATE0 = 0, Q_LORA, Q_LORA + KV_LORA, Q_LORA + KV_LORA + LANES
_WIN_COLS = _GATE0 + ATTN_W
_QN_COLS = N_HEADS * NOPE_DIM
_QR_COLS = N_HEADS * ROPE_DIM


def _silu(v):
    return v * jax.nn.sigmoid(v)


def _rms(v, g):
    return v * lax.rsqrt(jnp.mean(v * v, axis=-1, keepdims=True) + EPS) * g


def _dot(a, b):
    return jnp.dot(a, b, preferred_element_type=F32)


def _dot_nt(a, b):
    return lax.dot_general(a, b, (((1,), (1,)), ((), ())), preferred_element_type=F32)


def _rope(v, c, s1, s2):
    return v * c + pltpu.roll(v, 16, axis=1) * s1 + pltpu.roll(v, LANES - 16, axis=1) * s2


def _mod_row(ref, row):
    return ref[0, pl.ds(row, 1), :]


def _layer_spec(arr, idx):
    zeros = (0,) * (arr.ndim - 1)
    return pl.BlockSpec((1,) + arr.shape[1:], lambda i, t: (idx,) + zeros, pipeline_mode=pl.Buffered(1))


def _mod_spec(layer, part):
    return pl.BlockSpec((1, MOD_ROWS, D_MODEL), lambda i, t: (layer, 0, part))


def _tok_spec(tm, width):
    return pl.BlockSpec((1, tm, width), lambda i, t: (i, t, 0))


_PARAMS = pltpu.CompilerParams(vmem_limit_bytes=VMEM_LIMIT)


def _mod_kernel(c_ref, w_ref, b_ref, o_ref):
    sc = _silu(c_ref[...]).astype(BF16)
    o_ref[0] = _dot(sc, w_ref[0].astype(BF16)) + b_ref[0]


def _modulation(cvec, w_mod, b_mod):
    d = D_MODEL
    return pl.pallas_call(
        _mod_kernel,
        grid=(DEPTH,),
        in_specs=[
            pl.BlockSpec((MOD_ROWS, d), lambda i: (0, 0)),
            pl.BlockSpec((1, d, 3 * d), lambda i: (i, 0, 0)),
            pl.BlockSpec((1, 1, 3 * d), lambda i: (i, 0, 0)),
        ],
        out_specs=pl.BlockSpec((1, MOD_ROWS, 3 * d), lambda i: (i, 0, 0)),
        out_shape=jax.ShapeDtypeStruct((DEPTH, MOD_ROWS, 3 * d), F32),
        compiler_params=_PARAMS,
        name="modulation",
    )(cvec, w_mod, b_mod.reshape(DEPTH, 1, 3 * d))


def _mla_proj_kernel(ctx_row, x_ref, shift_ref, scale_ref, gpre_ref, win_ref, gq_ref, wuq_ref, gkv_ref,
                     wuk_ref, wuvt_ref, *rest):
    row = pl.program_id(0) if ctx_row is None else ctx_row
    if ctx_row is None:
        tc_ref, ts1_ref, ts2_ref, qn_ref, qr_ref, kn_ref, kr_ref, vt_ref, sg_ref = rest
        tc, ts1, ts2 = tc_ref[...], ts1_ref[...], ts2_ref[...]
        rope = lambda v: _rope(v, tc, ts1, ts2)
    else:
        qn_ref, qr_ref, kn_ref, kr_ref, vt_ref, sg_ref = rest
        rope = lambda v: v
    x = x_ref[0]
    h = _rms(x, gpre_ref[0]) * (1.0 + _mod_row(scale_ref, row)) + _mod_row(shift_ref, row)
    u = _dot(h.astype(BF16), win_ref[0])

    cqn = _rms(u[:, _CQ0:_CKV0], gq_ref[0]).astype(BF16)
    q = _dot(cqn, wuq_ref[0])
    for j in range(_QR_COLS // LANES):
        lo = _QN_COLS + j * LANES
        qr_ref[0, j] = rope(q[:, lo:lo + LANES]).astype(BF16)

    ckvn = _rms(u[:, _CKV0:_KR0], gkv_ref[0]).astype(BF16)
    kn = _dot(ckvn, wuk_ref[0])
    sg = _silu(u[:, _GATE0:])
    for hd in range(N_HEADS):
        hs = slice(hd * NOPE_DIM, (hd + 1) * NOPE_DIM)
        qn_ref[0, hd] = q[:, hs].astype(BF16)
        kn_ref[0, hd] = kn[:, hs].astype(BF16)
        sg_ref[0, hd] = sg[:, hs].astype(BF16)
    vt_ref[0] = _dot_nt(wuvt_ref[0], ckvn).astype(BF16)
    kr_ref[0] = rope(u[:, _KR0:_GATE0]).astype(BF16)


def _mla_proj(x, mod, layer, j, pm, tables, ctx_row, tm):
    b, l, d = x.shape
    assert (ctx_row is None) == (len(tables) == 3)
    tabspec = pl.BlockSpec((tm, LANES), lambda i, t: (t, 0))
    heads = lambda n: pl.BlockSpec((1, n, tm, LANES), lambda i, t: (i, 0, t, 0))
    hshape = lambda n: jax.ShapeDtypeStruct((b, n, l, LANES), BF16)
    nh, npair = N_HEADS, N_HEADS // _HEAD_PAIR
    weights = [pm["w_in"], pm["g_q"], pm["w_uq"], pm["g_kv"], pm["w_uk"], pm["w_uvt"]]
    return pl.pallas_call(
        functools.partial(_mla_proj_kernel, ctx_row),
        grid=(b, l // tm),
        in_specs=[_tok_spec(tm, d), _mod_spec(layer, 0), _mod_spec(layer, 1), _layer_spec(pm["g_pre"], layer)]
                 + [_layer_spec(w, j) for w in weights] + [tabspec] * len(tables),
        out_specs=[heads(nh), heads(npair), heads(nh), _tok_spec(tm, LANES),
                   pl.BlockSpec((1, ATTN_W, tm), lambda i, t: (i, 0, t)), heads(nh)],
        out_shape=[hshape(nh), hshape(npair), hshape(nh), jax.ShapeDtypeStruct((b, l, LANES), BF16),
                   jax.ShapeDtypeStruct((b, ATTN_W, l), BF16), hshape(nh)],
        compiler_params=_PARAMS,
        name="mla_proj",
    )(x, mod, mod, pm["g_pre"], *weights, *tables)


def _attn_kernel(seg_lens, tq, qn_ref, qr_ref, sg_ref, *rest):
    nseg = len(seg_lens)
    seg_refs = rest[:3 * nseg]
    o_ref, k_scr, vt_scr, st_scr, m_scr, p_scr = rest[3 * nseg:]
    lane_half = lax.broadcasted_iota(jnp.int32, (1, LANES), 1) // ROPE_DIM
    hp = qn_ref.shape[1]
    nt = qn_ref.shape[2] // tq
    n_units = hp * nt
    assert n_units >= 2

    for hh in range(hp):
        off = 0
        for s, ls in enumerate(seg_lens):
            kn_ref, kr_ref, vt_ref = seg_refs[3 * s:3 * s + 3]
            k_scr[hh, off:off + ls, :NOPE_DIM] = kn_ref[0, hh]
            in_half = lane_half == hh % _HEAD_PAIR
            k_scr[hh, off:off + ls, NOPE_DIM:] = jnp.where(in_half, kr_ref[0], jnp.zeros((), BF16))
            vt_scr[hh, :V_DIM, off:off + ls] = vt_ref[0, hh * V_DIM:(hh + 1) * V_DIM, :]
            off += ls
        vt_scr[hh, V_DIM:, :] = jnp.ones((_ONES_ROWS, off), BF16)

    def unit(u):
        return u // nt, pl.ds(pl.multiple_of((u % nt) * tq, tq), tq)

    def scores(u):
        hh, rows = unit(u)
        q = jnp.concatenate([qn_ref[0, hh, rows, :], qr_ref[0, hh // _HEAD_PAIR, rows, :]], axis=1)
        st = _dot_nt(k_scr[hh], q)
        st_scr[...] = st
        m_scr[...] = jnp.max(st, axis=0, keepdims=True)

    def probs():
        m = m_scr[...]
        for c in range(0, st_scr.shape[0], _PROB_ROWS):
            rows = slice(c, c + _PROB_ROWS)
            p_scr[rows, :] = jnp.exp2((st_scr[rows, :] - m).astype(BF16))

    def values(u):
        hh, rows = unit(u)
        ot = _dot(vt_scr[hh], p_scr[...])
        ot = ot[:V_DIM] / ot[V_DIM:V_DIM + 1]
        o_ref[0, hh, rows, :] = (ot.T * sg_ref[0, hh, rows, :].astype(F32)).astype(BF16)

    scores(0)
    probs()
    scores(1)

    def iteration(i, carry):
        values(i - 2)
        probs()
        scores(i)
        return carry

    lax.fori_loop(2, n_units, iteration, 0)
    values(n_units - 2)
    probs()
    values(n_units - 1)


def _attention(qn, qr, sg, segments, tq, hp):
    b, nh, lq, _ = qn.shape
    seg_lens = tuple(s[0].shape[2] for s in segments)
    lk = sum(seg_lens)
    assert hp % _HEAD_PAIR == 0 and nh % hp == 0
    pair = lambda l: pl.BlockSpec((1, hp, l, LANES), lambda i, h: (i, h, 0, 0))
    in_specs = [pair(lq), pl.BlockSpec((1, hp // _HEAD_PAIR, lq, LANES), lambda i, h: (i, h, 0, 0)), pair(lq)]
    args = [qn, qr, sg]
    for kn, kr, vt in segments:
        ls = kn.shape[2]
        in_specs += [pair(ls), pl.BlockSpec((1, ls, LANES), lambda i, h: (i, 0, 0)),
                     pl.BlockSpec((1, hp * V_DIM, ls), lambda i, h: (i, h, 0))]
        args += [kn, kr, vt]
    return pl.pallas_call(
        functools.partial(_attn_kernel, seg_lens, tq),
        grid=(b, nh // hp),
        in_specs=in_specs,
        out_specs=pair(lq),
        out_shape=jax.ShapeDtypeStruct((b, nh, lq, LANES), BF16),
        scratch_shapes=[pltpu.VMEM((hp, lk, NOPE_DIM + LANES), BF16),
                        pltpu.VMEM((hp, V_DIM + _ONES_ROWS, lk), BF16),
                        pltpu.VMEM((lk, tq), F32), pltpu.VMEM((1, tq), F32), pltpu.VMEM((lk, tq), BF16)],
        compiler_params=_PARAMS,
        name="attention",
    )(*args)


def _conv_in_math(x, row, rows, shift_ref, scale_ref, gpre_ref, win_ref, u_ref, sg_ref):
    h = _rms(x, gpre_ref[0]) * (1.0 + _mod_row(scale_ref, row)) + _mod_row(shift_ref, row)
    u3 = _dot(h.astype(BF16), win_ref[0])
    u_ref[0, rows, :] = (u3[:, :CONV_W] * jax.nn.sigmoid(u3[:, CONV_W:2 * CONV_W])).astype(BF16)
    sg_ref[0, rows, :] = _silu(u3[:, 2 * CONV_W:]).astype(BF16)


def _row_parts(tm):
    n = _TILE_PARTS if tm % (_TILE_PARTS * LANES) == 0 else 1
    return [slice(i * (tm // n), (i + 1) * (tm // n)) for i in range(n)]


def _out_kernel(ctx_row, fuse_conv_in, og_ref, x_ref, gate_ref, wo_ref, gpost_ref, *rest):
    row = pl.program_id(0) if ctx_row is None else ctx_row
    parts = _row_parts(x_ref.shape[1])
    ys = []
    for rows in parts:
        og = jnp.concatenate([og_ref[0, hd, rows, :] for hd in range(og_ref.shape[1])], axis=1)
        ys.append(_dot(og, wo_ref[0]))
    for rows, y in zip(parts, ys):
        x_new = x_ref[0, rows, :] + _mod_row(gate_ref, row) * _rms(y, gpost_ref[0])
        if fuse_conv_in:
            shift_ref, scale_ref, gpre_ref, win_ref, o_ref, u_ref, sg_ref = rest
            _conv_in_math(x_new, row, rows, shift_ref, scale_ref, gpre_ref, win_ref, u_ref, sg_ref)
        else:
            o_ref, = rest
        o_ref[0, rows, :] = x_new


def _out_residual(og, x, mod, layer, j, pm, ctx_row, tm, next_conv=None):
    b, l, d = x.shape
    in_specs = [pl.BlockSpec((1, og.shape[1], tm, LANES), lambda i, t: (i, 0, t, 0)), _tok_spec(tm, d),
                _mod_spec(layer, 2), _layer_spec(pm["w_o"], j), _layer_spec(pm["g_post"], layer)]
    args = [og, x, mod, pm["w_o"], pm["g_post"]]
    out_specs, out_shape = [_tok_spec(tm, d)], [jax.ShapeDtypeStruct((b, l, d), F32)]
    if next_conv is not None:
        pc, jc = next_conv
        in_specs += [_mod_spec(layer + 1, 0), _mod_spec(layer + 1, 1), _layer_spec(pc["g_pre"], layer + 1),
                     _layer_spec(pc["w_in"], jc)]
        args += [mod, mod, pc["g_pre"], pc["w_in"]]
        out_specs += [_tok_spec(tm, CONV_W)] * 2
        out_shape += [jax.ShapeDtypeStruct((b, l, CONV_W), BF16)] * 2
    res = pl.pallas_call(
        functools.partial(_out_kernel, ctx_row, next_conv is not None),
        grid=(b, l // tm),
        in_specs=in_specs,
        out_specs=out_specs,
        out_shape=out_shape,
        compiler_params=_PARAMS,
        name="out_residual",
    )(*args)
    return res if next_conv is not None else res[0]


def _conv_in_kernel(ctx_row, x_ref, shift_ref, scale_ref, gpre_ref, win_ref, u_ref, sg_ref):
    row = pl.program_id(0) if ctx_row is None else ctx_row
    for rows in _row_parts(x_ref.shape[1]):
        _conv_in_math(x_ref[0, rows, :], row, rows, shift_ref, scale_ref, gpre_ref, win_ref, u_ref, sg_ref)


def _conv_in(x, mod, layer, j, pc, ctx_row, tm):
    b, l, d = x.shape
    return pl.pallas_call(
        functools.partial(_conv_in_kernel, ctx_row),
        grid=(b, l // tm),
        in_specs=[_tok_spec(tm, d), _mod_spec(layer, 0), _mod_spec(layer, 1), _layer_spec(pc["g_pre"], layer),
                  _layer_spec(pc["w_in"], j)],
        out_specs=[_tok_spec(tm, CONV_W)] * 2,
        out_shape=[jax.ShapeDtypeStruct((b, l, CONV_W), BF16)] * 2,
        compiler_params=_PARAMS,
        name="conv_in",
    )(x, mod, mod, pc["g_pre"], pc["w_in"])


def _conv_out_kernel(ctx_row, tm, u_ref, sg_ref, x_ref, gate_ref, wdw_ref, bdw_ref, gln_ref, bln_ref,
                     wo_ref, gpost_ref, o_ref, win, cv):
    row = pl.program_id(0) if ctx_row is None else ctx_row
    t = pl.program_id(1)
    nt = pl.num_programs(1)
    l = u_ref.shape[1]
    r0 = pl.multiple_of(t * tm, tm)

    top0 = pl.multiple_of(jnp.maximum(r0 - _HALO, 0), _HALO)
    bot0 = pl.multiple_of(jnp.minimum(r0 + tm, l - _HALO), _HALO)
    top = u_ref[0, pl.ds(top0, _HALO), :].astype(F32) * (t > 0).astype(F32)
    mid = u_ref[0, pl.ds(r0, tm), :].astype(F32)
    bot = u_ref[0, pl.ds(bot0, _HALO), :].astype(F32) * (t < nt - 1).astype(F32)
    for lc in range(CONV_W // LANES):
        ls = slice(lc * LANES, (lc + 1) * LANES)
        win[lc, 0:_HALO] = top[:, ls]
        win[lc, _HALO:_HALO + tm] = mid[:, ls]
        win[lc, _HALO + tm:] = bot[:, ls]

    for rc in range(tm // _CONV_ROWS):
        for lc in range(CONV_W // LANES):
            ls = slice(lc * LANES, (lc + 1) * LANES)
            acc = jnp.broadcast_to(bdw_ref[0, :, ls], (_CONV_ROWS, LANES))
            for k in range(CONV_K):
                lo = rc * _CONV_ROWS + k + _HALO - CONV_PAD
                acc = acc + win[lc, lo:lo + _CONV_ROWS, :] * wdw_ref[0, k:k + 1, ls]
            cv[rc * _CONV_ROWS:(rc + 1) * _CONV_ROWS, ls] = acc

    c = cv[...]
    mu = jnp.mean(c, axis=-1, keepdims=True)
    cc = c - mu
    var = jnp.mean(cc * cc, axis=-1, keepdims=True)
    ln = cc * lax.rsqrt(var + EPS) * gln_ref[0] + bln_ref[0]
    z = (_silu(ln) * sg_ref[0].astype(F32)).astype(BF16)
    y = _dot(z, wo_ref[0])
    o_ref[0] = x_ref[0] + _mod_row(gate_ref, row) * _rms(y, gpost_ref[0])


def _conv_out(u, sg, x, mod, layer, j, pc, ctx_row, tm):
    b, l, d = x.shape
    per_conv = [pc["w_dw"], pc["b_dw"], pc["g_ln"], pc["b_ln"], pc["w_o"]]
    return pl.pallas_call(
        functools.partial(_conv_out_kernel, ctx_row, tm),
        grid=(b, l // tm),
        in_specs=[pl.BlockSpec((1, l, CONV_W), lambda i, t: (i, 0, 0)), _tok_spec(tm, CONV_W), _tok_spec(tm, d),
                  _mod_spec(layer, 2)] + [_layer_spec(w, j) for w in per_conv] + [_layer_spec(pc["g_post"], layer)],
        out_specs=_tok_spec(tm, d),
        out_shape=jax.ShapeDtypeStruct((b, l, d), F32),
        scratch_shapes=[pltpu.VMEM((CONV_W // LANES, tm + 2 * _HALO, LANES), F32), pltpu.VMEM((tm, CONV_W), F32)],
        compiler_params=_PARAMS,
        name="conv_out",
    )(u, sg, x, mod, *per_conv, pc["g_post"])


def _rope_tables(n_tokens):
    f32 = np.float32
    rows_n = n_tokens // GRID_W
    rows = np.repeat(np.arange(rows_n, dtype=f32), GRID_W)
    cols = np.tile(np.arange(GRID_W, dtype=f32), rows_n)
    half = ROPE_DIM // 2
    freqs = (f32(1.0) / (f32(ROPE_BASE) ** (np.arange(0, half, 2, dtype=f32) / f32(half)))).astype(f32)
    ang_r = rows[:, None] * freqs[None, :]
    ang_c = cols[:, None] * freqs[None, :]
    zero = np.zeros_like(ang_r)
    cos64 = np.concatenate([np.cos(ang_r), np.cos(ang_r), np.cos(ang_c), np.cos(ang_c)], axis=-1)
    s1_64 = np.concatenate([zero, np.sin(ang_r), zero, np.sin(ang_c)], axis=-1)
    s2_64 = np.concatenate([-np.sin(ang_r), zero, -np.sin(ang_c), zero], axis=-1)
    dup = lambda a: jnp.asarray(np.concatenate([a, a], axis=-1).astype(f32))
    return dup(cos64), dup(s1_64), dup(s2_64)


def _mla_params(g_pre, g_post, mla_w_in, mla_g_q, mla_w_uq, mla_g_kv, mla_w_ukv, mla_w_o):
    n = mla_w_in.shape[0]
    kr = mla_w_in[:, :, _KR0:_KR0 + ROPE_DIM]
    w_in = jnp.concatenate([mla_w_in[:, :, :_KR0], kr, kr, mla_w_in[:, :, _KR0 + ROPE_DIM:]], axis=2).astype(BF16)
    w_uq = mla_w_uq.reshape(n, Q_LORA, N_HEADS, NOPE_DIM + ROPE_DIM)
    w_uq = jnp.concatenate([w_uq[..., :NOPE_DIM].reshape(n, Q_LORA, _QN_COLS),
                            w_uq[..., NOPE_DIM:].reshape(n, Q_LORA, _QR_COLS)], axis=2).astype(BF16)
    w_ukv = mla_w_ukv.reshape(n, KV_LORA, N_HEADS, NOPE_DIM + V_DIM)
    w_uk = w_ukv[..., :NOPE_DIM].reshape(n, KV_LORA, _QN_COLS).astype(BF16)
    w_uvt = jnp.swapaxes(w_ukv[..., NOPE_DIM:].reshape(n, KV_LORA, ATTN_W), 1, 2).astype(BF16)
    scale = math.log2(math.e) / math.sqrt(NOPE_DIM + ROPE_DIM)
    return dict(g_pre=g_pre, g_post=g_post, w_in=w_in, g_q=(mla_g_q * scale)[:, None, :], w_uq=w_uq,
                g_kv=mla_g_kv[:, None, :], w_uk=w_uk, w_uvt=w_uvt, w_o=mla_w_o.astype(BF16))


def kernel(x, c, ctx, c_ctx, w_mod, b_mod, g_pre, g_post, mla_w_in, mla_g_q, mla_w_uq, mla_g_kv, mla_w_ukv,
           mla_w_o, cv_w_in, cv_w_dw, cv_b_dw, cv_g_ln, cv_b_ln, cv_w_o):
    b, s, d = x.shape
    lc = ctx.shape[1]
    ctx_row = b
    assert b < MOD_ROWS and s % GRID_W == 0

    cvec = jnp.zeros((MOD_ROWS, d), F32).at[:b].set(c).at[ctx_row].set(c_ctx)
    mod = _modulation(cvec, w_mod, b_mod)

    tables_x = _rope_tables(s)
    g_pre3, g_post3 = g_pre[:, None, :], g_post[:, None, :]
    pm = _mla_params(g_pre3, g_post3, mla_w_in, mla_g_q, mla_w_uq, mla_g_kv, mla_w_ukv, mla_w_o)
    pc = dict(g_pre=g_pre3, g_post=g_post3, w_in=cv_w_in.astype(BF16), w_dw=cv_w_dw.reshape(-1, CONV_K, CONV_W),
              b_dw=cv_b_dw[:, None, :], g_ln=cv_g_ln[:, None, :], b_ln=cv_b_ln[:, None, :], w_o=cv_w_o.astype(BF16))

    conv_inputs = None
    for i in range(DEPTH):
        last = i == DEPTH - 1
        j = i // N_MIXERS
        if i % N_MIXERS == 0:
            next_conv = (pc, (i + 1) // N_MIXERS) if not last and (i + 1) % N_MIXERS == 1 else None
            qn, qr, kn, kr, vt, sg = _mla_proj(x, mod, i, j, pm, tables_x, None, _TM_MLA)
            qn_c, qr_c, kn_c, kr_c, vt_c, sg_c = _mla_proj(ctx, mod, i, j, pm, (), ctx_row, lc)
            og = _attention(qn, qr, sg, [(kn, kr, vt), (kn_c, kr_c, vt_c)], _TQ, _HEADS_X)
            res_x = _out_residual(og, x, mod, i, j, pm, None, _TM_PROJ, next_conv)
            res_c = None
            if not last:
                og_c = _attention(qn_c, qr_c, sg_c, [(kn_c, kr_c, vt_c)], lc, N_HEADS)
                res_c = _out_residual(og_c, ctx, mod, i, j, pm, ctx_row, lc, next_conv)
            if next_conv is None:
                x, ctx = res_x, (ctx if res_c is None else res_c)
            else:
                x, ctx = res_x[0], res_c[0]
                conv_inputs = (res_x[1:], res_c[1:])
        else:
            if conv_inputs is None:
                conv_inputs = (_conv_in(x, mod, i, j, pc, None, _TM_PROJ),
                               _conv_in(ctx, mod, i, j, pc, ctx_row, lc) if not last else None)
            (u, sg), uc_sgc = conv_inputs
            conv_inputs = None
            x = _conv_out(u, sg, x, mod, i, j, pc, None, _TM_CONV)
            if not last:
                ctx = _conv_out(uc_sgc[0], uc_sgc[1], ctx, mod, i, j, pc, ctx_row, lc)
    return x
```

```python
import functools
import math

import jax
import jax.numpy as jnp
import numpy as np
from jax import lax
from jax.experimental import pallas as pl
from jax.experimental.pallas import tpu as pltpu

D_MODEL = 1024
DEPTH = 4
GRID_W = 64
N_MIXERS = 2
N_HEADS = 8
NOPE_DIM = 128
ROPE_DIM = 64
V_DIM = 128
Q_LORA = 384
KV_LORA = 256
ATTN_W = N_HEADS * V_DIM
ROPE_BASE = 10000.0
CONV_W = D_MODEL
CONV_K = 31
CONV_PAD = CONV_K // 2
EPS = 1e-6

LANES = 128
MOD_ROWS = 16
VMEM_LIMIT = 56 * 1024 * 1024
_HEAD_PAIR = 2
_HEADS_X = 2
_ONES_ROWS = 16
_PROB_ROWS = 64
_TM_MLA = 512
_TM_PROJ = 1024
_TM_CONV = 1024
_TILE_PARTS = 4
_TQ = 1024
_HALO = 16
_CONV_ROWS = 128
assert _HALO >= CONV_PAD + 1

F32 = jnp.float32
BF16 = jnp.bfloat16

_CQ0, _CKV0, _KR0, _GATE0 = 0, Q_LORA, Q_LORA + KV_LORA, Q_LORA + KV_LORA + LANES
_WIN_COLS = _GATE0 + ATTN_W
_QN_COLS = N_HEADS * NOPE_DIM
_QR_COLS = N_HEADS * ROPE_DIM


def _silu(v):
    return v * jax.nn.sigmoid(v)


def _rms(v, g):
    return v * lax.rsqrt(jnp.mean(v * v, axis=-1, keepdims=True) + EPS) * g


def _dot(a, b):
    return jnp.dot(a, b, preferred_element_type=F32)


def _dot_nt(a, b):
    return lax.dot_general(a, b, (((1,), (1,)), ((), ())), preferred_element_type=F32)


def _rope(v, c, s1, s2):
    return v * c + pltpu.roll(v, 16, axis=1) * s1 + pltpu.roll(v, LANES - 16, axis=1) * s2


def _mod_row(ref, row):
    return ref[0, pl.ds(row, 1), :]


def _layer_spec(arr, idx):
    zeros = (0,) * (arr.ndim - 1)
    return pl.BlockSpec((1,) + arr.shape[1:], lambda i, t: (idx,) + zeros, pipeline_mode=pl.Buffered(1))


def _mod_spec(layer, part):
    return pl.BlockSpec((1, MOD_ROWS, D_MODEL), lambda i, t: (layer, 0, part))


def _tok_spec(tm, width):
    return pl.BlockSpec((1, tm, width), lambda i, t: (i, t, 0))


_PARAMS = pltpu.CompilerParams(vmem_limit_bytes=VMEM_LIMIT)


def _mod_kernel(c_ref, w_ref, b_ref, o_ref):
    sc = _silu(c_ref[...]).astype(BF16)
    o_ref[0] = _dot(sc, w_ref[0].astype(BF16)) + b_ref[0]


def _modulation(cvec, w_mod, b_mod):
    d = D_MODEL
    return pl.pallas_call(
        _mod_kernel,
        grid=(DEPTH,),
        in_specs=[
            pl.BlockSpec((MOD_ROWS, d), lambda i: (0, 0)),
            pl.BlockSpec((1, d, 3 * d), lambda i: (i, 0, 0)),
            pl.BlockSpec((1, 1, 3 * d), lambda i: (i, 0, 0)),
        ],
        out_specs=pl.BlockSpec((1, MOD_ROWS, 3 * d), lambda i: (i, 0, 0)),
        out_shape=jax.ShapeDtypeStruct((DEPTH, MOD_ROWS, 3 * d), F32),
        compiler_params=_PARAMS,
        name="modulation",
    )(cvec, w_mod, b_mod.reshape(DEPTH, 1, 3 * d))


def _mla_proj_kernel(ctx_row, x_ref, shift_ref, scale_ref, gpre_ref, win_ref, gq_ref, wuq_ref, gkv_ref,
                     wuk_ref, wuvt_ref, *rest):
    row = pl.program_id(0) if ctx_row is None else ctx_row
    if ctx_row is None:
        tc_ref, ts1_ref, ts2_ref, qn_ref, qr_ref, kn_ref, kr_ref, vt_ref, sg_ref = rest
        tc, ts1, ts2 = tc_ref[...], ts1_ref[...], ts2_ref[...]
        rope = lambda v: _rope(v, tc, ts1, ts2)
    else:
        qn_ref, qr_ref, kn_ref, kr_ref, vt_ref, sg_ref = rest
        rope = lambda v: v
    x = x_ref[0]
    h = _rms(x, gpre_ref[0]) * (1.0 + _mod_row(scale_ref, row)) + _mod_row(shift_ref, row)
    u = _dot(h.astype(BF16), win_ref[0])

    cqn = _rms(u[:, _CQ0:_CKV0], gq_ref[0]).astype(BF16)
    q = _dot(cqn, wuq_ref[0])
    for j in range(_QR_COLS // LANES):
        lo = _QN_COLS + j * LANES
        qr_ref[0, j] = rope(q[:, lo:lo + LANES]).astype(BF16)

    ckvn = _rms(u[:, _CKV0:_KR0], gkv_ref[0]).astype(BF16)
    kn = _dot(ckvn, wuk_ref[0])
    sg = _silu(u[:, _GATE0:])
    for hd in range(N_HEADS):
        hs = slice(hd * NOPE_DIM, (hd + 1) * NOPE_DIM)
        qn_ref[0, hd] = q[:, hs].astype(BF16)
        kn_ref[0, hd] = kn[:, hs].astype(BF16)
        sg_ref[0, hd] = sg[:, hs].astype(BF16)
    vt_ref[0] = _dot_nt(wuvt_ref[0], ckvn).astype(BF16)
    kr_ref[0] = rope(u[:, _KR0:_GATE0]).astype(BF16)


def _mla_proj(x, mod, layer, j, pm, tables, ctx_row, tm):
    b, l, d = x.shape
    assert (ctx_row is None) == (len(tables) == 3)
    tabspec = pl.BlockSpec((tm, LANES), lambda i, t: (t, 0))
    heads = lambda n: pl.BlockSpec((1, n, tm, LANES), lambda i, t: (i, 0, t, 0))
    hshape = lambda n: jax.ShapeDtypeStruct((b, n, l, LANES), BF16)
    nh, npair = N_HEADS, N_HEADS // _HEAD_PAIR
    weights = [pm["w_in"], pm["g_q"], pm["w_uq"], pm["g_kv"], pm["w_uk"], pm["w_uvt"]]
    return pl.pallas_call(
        functools.partial(_mla_proj_kernel, ctx_row),
        grid=(b, l // tm),
        in_specs=[_tok_spec(tm, d), _mod_spec(layer, 0), _mod_spec(layer, 1), _layer_spec(pm["g_pre"], layer)]
                 + [_layer_spec(w, j) for w in weights] + [tabspec] * len(tables),
        out_specs=[heads(nh), heads(npair), heads(nh), _tok_spec(tm, LANES),
                   pl.BlockSpec((1, ATTN_W, tm), lambda i, t: (i, 0, t)), heads(nh)],
        out_shape=[hshape(nh), hshape(npair), hshape(nh), jax.ShapeDtypeStruct((b, l, LANES), BF16),
                   jax.ShapeDtypeStruct((b, ATTN_W, l), BF16), hshape(nh)],
        compiler_params=_PARAMS,
        name="mla_proj",
    )(x, mod, mod, pm["g_pre"], *weights, *tables)


def _attn_kernel(seg_lens, tq, qn_ref, qr_ref, sg_ref, *rest):
    nseg = len(seg_lens)
    seg_refs = rest[:3 * nseg]
    o_ref, k_scr, vt_scr, st_scr, m_scr, p_scr = rest[3 * nseg:]
    lane_half = lax.broadcasted_iota(jnp.int32, (1, LANES), 1) // ROPE_DIM
    hp = qn_ref.shape[1]
    nt = qn_ref.shape[2] // tq
    n_units = hp * nt
    assert n_units >= 2

    for hh in range(hp):
        off = 0
        for s, ls in enumerate(seg_lens):
            kn_ref, kr_ref, vt_ref = seg_refs[3 * s:3 * s + 3]
            k_scr[hh, off:off + ls, :NOPE_DIM] = kn_ref[0, hh]
            in_half = lane_half == hh % _HEAD_PAIR
            k_scr[hh, off:off + ls, NOPE_DIM:] = jnp.where(in_half, kr_ref[0], jnp.zeros((), BF16))
            vt_scr[hh, :V_DIM, off:off + ls] = vt_ref[0, hh * V_DIM:(hh + 1) * V_DIM, :]
            off += ls
        vt_scr[hh, V_DIM:, :] = jnp.ones((_ONES_ROWS, off), BF16)

    def unit(u):
        return u // nt, pl.ds(pl.multiple_of((u % nt) * tq, tq), tq)

    def scores(u):
        hh, rows = unit(u)
        q = jnp.concatenate([qn_ref[0, hh, rows, :], qr_ref[0, hh // _HEAD_PAIR, rows, :]], axis=1)
        st = _dot_nt(k_scr[hh], q)
        st_scr[...] = st
        m_scr[...] = jnp.max(st, axis=0, keepdims=True)

    def probs():
        m = m_scr[...]
        for c in range(0, st_scr.shape[0], _PROB_ROWS):
            rows = slice(c, c + _PROB_ROWS)
            p_scr[rows, :] = jnp.exp2((st_scr[rows, :] - m).astype(BF16))

    def values(u):
        hh, rows = unit(u)
        ot = _dot(vt_scr[hh], p_scr[...])
        ot = ot[:V_DIM] / ot[V_DIM:V_DIM + 1]
        o_ref[0, hh, rows, :] = (ot.T * sg_ref[0, hh, rows, :].astype(F32)).astype(BF16)

    scores(0)
    probs()
    scores(1)

    def iteration(i, carry):
        values(i - 2)
        probs()
        scores(i)
        return carry

    lax.fori_loop(2, n_units, iteration, 0)
    values(n_units - 2)
    probs()
    values(n_units - 1)


def _attention(qn, qr, sg, segments, tq, hp):
    b, nh, lq, _ = qn.shape
    seg_lens = tuple(s[0].shape[2] for s in segments)
    lk = sum(seg_lens)
    assert hp % _HEAD_PAIR == 0 and nh % hp == 0
    pair = lambda l: pl.BlockSpec((1, hp, l, LANES), lambda i, h: (i, h, 0, 0))
    in_specs = [pair(lq), pl.BlockSpec((1, hp // _HEAD_PAIR, lq, LANES), lambda i, h: (i, h, 0, 0)), pair(lq)]
    args = [qn, qr, sg]
    for kn, kr, vt in segments:
        ls = kn.shape[2]
        in_specs += [pair(ls), pl.BlockSpec((1, ls, LANES), lambda i, h: (i, 0, 0)),
                     pl.BlockSpec((1, hp * V_DIM, ls), lambda i, h: (i, h, 0))]
        args += [kn, kr, vt]
    return pl.pallas_call(
        functools.partial(_attn_kernel, seg_lens, tq),
        grid=(b, nh // hp),
        in_specs=in_specs,
        out_specs=pair(lq),
        out_shape=jax.ShapeDtypeStruct((b, nh, lq, LANES), BF16),
        scratch_shapes=[pltpu.VMEM((hp, lk, NOPE_DIM + LANES), BF16),
                        pltpu.VMEM((hp, V_DIM + _ONES_ROWS, lk), BF16),
                        pltpu.VMEM((lk, tq), F32), pltpu.VMEM((1, tq), F32), pltpu.VMEM((lk, tq), BF16)],
        compiler_params=_PARAMS,
        name="attention",
    )(*args)


def _ctx_attn_kernel(qn_ref, qr_ref, sg_ref, kn_ref, kr_ref, vt_ref, o_ref):
    lane_half = lax.broadcasted_iota(jnp.int32, (1, LANES), 1) // ROPE_DIM
    lk = kn_ref.shape[2]
    ones = jnp.ones((_ONES_ROWS, lk), BF16)
    for hd in range(N_HEADS):
        kr = jnp.where(lane_half == hd % _HEAD_PAIR, kr_ref[0], jnp.zeros((), BF16))
        k = jnp.concatenate([kn_ref[0, hd], kr], axis=1)
        q = jnp.concatenate([qn_ref[0, hd], qr_ref[0, hd // _HEAD_PAIR]], axis=1)
        st = _dot_nt(k, q)
        p = jnp.exp2((st - jnp.max(st, axis=0, keepdims=True)).astype(BF16))
        vte = jnp.concatenate([vt_ref[0, hd * V_DIM:(hd + 1) * V_DIM, :], ones], axis=0)
        ot = _dot(vte, p)
        ot = ot[:V_DIM] / ot[V_DIM:V_DIM + 1]
        o_ref[0, hd] = (ot.T * sg_ref[0, hd].astype(F32)).astype(BF16)


def _ctx_attention(qn, qr, sg, kn, kr, vt):
    b, nh, l, _ = qn.shape
    heads = lambda n: pl.BlockSpec((1, n, l, LANES), lambda i: (i, 0, 0, 0))
    return pl.pallas_call(
        _ctx_attn_kernel,
        grid=(b,),
        in_specs=[heads(nh), heads(nh // _HEAD_PAIR), heads(nh), heads(nh),
                  pl.BlockSpec((1, l, LANES), lambda i: (i, 0, 0)), pl.BlockSpec((1, nh * V_DIM, l), lambda i: (i, 0, 0))],
        out_specs=heads(nh),
        out_shape=jax.ShapeDtypeStruct((b, nh, l, LANES), BF16),
        compiler_params=_PARAMS,
        name="ctx_attention",
    )(qn, qr, sg, kn, kr, vt)


def _conv_in_math(x, row, rows, shift_ref, scale_ref, gpre_ref, win_ref, u_ref, sg_ref):
    h = _rms(x, gpre_ref[0]) * (1.0 + _mod_row(scale_ref, row)) + _mod_row(shift_ref, row)
    u3 = _dot(h.astype(BF16), win_ref[0])
    u_ref[0, rows, :] = (u3[:, :CONV_W] * jax.nn.sigmoid(u3[:, CONV_W:2 * CONV_W])).astype(BF16)
    sg_ref[0, rows, :] = _silu(u3[:, 2 * CONV_W:]).astype(BF16)


def _row_parts(tm):
    n = _TILE_PARTS if tm % (_TILE_PARTS * LANES) == 0 else 1
    return [slice(i * (tm // n), (i + 1) * (tm // n)) for i in range(n)]


def _out_kernel(ctx_row, fuse_conv_in, og_ref, x_ref, gate_ref, wo_ref, gpost_ref, *rest):
    row = pl.program_id(0) if ctx_row is None else ctx_row
    parts = _row_parts(x_ref.shape[1])
    ys = []
    for rows in parts:
        og = jnp.concatenate([og_ref[0, hd, rows, :] for hd in range(og_ref.shape[1])], axis=1)
        ys.append(_dot(og, wo_ref[0]))
    for rows, y in zip(parts, ys):
        x_new = x_ref[0, rows, :] + _mod_row(gate_ref, row) * _rms(y, gpost_ref[0])
        if fuse_conv_in:
            shift_ref, scale_ref, gpre_ref, win_ref, o_ref, u_ref, sg_ref = rest
            _conv_in_math(x_new, row, rows, shift_ref, scale_ref, gpre_ref, win_ref, u_ref, sg_ref)
        else:
            o_ref, = rest
        o_ref[0, rows, :] = x_new


def _out_residual(og, x, mod, layer, j, pm, ctx_row, tm, next_conv=None):
    b, l, d = x.shape
    in_specs = [pl.BlockSpec((1, og.shape[1], tm, LANES), lambda i, t: (i, 0, t, 0)), _tok_spec(tm, d),
                _mod_spec(layer, 2), _layer_spec(pm["w_o"], j), _layer_spec(pm["g_post"], layer)]
    args = [og, x, mod, pm["w_o"], pm["g_post"]]
    out_specs, out_shape = [_tok_spec(tm, d)], [jax.ShapeDtypeStruct((b, l, d), F32)]
    if next_conv is not None:
        pc, jc = next_conv
        in_specs += [_mod_spec(layer + 1, 0), _mod_spec(layer + 1, 1), _layer_spec(pc["g_pre"], layer + 1),
                     _layer_spec(pc["w_in"], jc)]
        args += [mod, mod, pc["g_pre"], pc["w_in"]]
        out_specs += [_tok_spec(tm, CONV_W)] * 2
        out_shape += [jax.ShapeDtypeStruct((b, l, CONV_W), BF16)] * 2
    res = pl.pallas_call(
        functools.partial(_out_kernel, ctx_row, next_conv is not None),
        grid=(b, l // tm),
        in_specs=in_specs,
        out_specs=out_specs,
        out_shape=out_shape,
        compiler_params=_PARAMS,
        name="out_residual",
    )(*args)
    return res if next_conv is not None else res[0]


def _conv_in_kernel(ctx_row, x_ref, shift_ref, scale_ref, gpre_ref, win_ref, u_ref, sg_ref):
    row = pl.program_id(0) if ctx_row is None else ctx_row
    for rows in _row_parts(x_ref.shape[1]):
        _conv_in_math(x_ref[0, rows, :], row, rows, shift_ref, scale_ref, gpre_ref, win_ref, u_ref, sg_ref)


def _conv_in(x, mod, layer, j, pc, ctx_row, tm):
    b, l, d = x.shape
    return pl.pallas_call(
        functools.partial(_conv_in_kernel, ctx_row),
        grid=(b, l // tm),
        in_specs=[_tok_spec(tm, d), _mod_spec(layer, 0), _mod_spec(layer, 1), _layer_spec(pc["g_pre"], layer),
                  _layer_spec(pc["w_in"], j)],
        out_specs=[_tok_spec(tm, CONV_W)] * 2,
        out_shape=[jax.ShapeDtypeStruct((b, l, CONV_W), BF16)] * 2,
        compiler_params=_PARAMS,
        name="conv_in",
    )(x, mod, mod, pc["g_pre"], pc["w_in"])


def _conv_out_kernel(ctx_row, tm, u_ref, sg_ref, x_ref, gate_ref, wdw_ref, bdw_ref, gln_ref, bln_ref,
                     wo_ref, gpost_ref, o_ref, win, cv):
    row = pl.program_id(0) if ctx_row is None else ctx_row
    t = pl.program_id(1)
    nt = pl.num_programs(1)
    l = u_ref.shape[1]
    r0 = pl.multiple_of(t * tm, tm)

    top0 = pl.multiple_of(jnp.maximum(r0 - _HALO, 0), _HALO)
    bot0 = pl.multiple_of(jnp.minimum(r0 + tm, l - _HALO), _HALO)
    top = u_ref[0, pl.ds(top0, _HALO), :].astype(F32) * (t > 0).astype(F32)
    mid = u_ref[0, pl.ds(r0, tm), :].astype(F32)
    bot = u_ref[0, pl.ds(bot0, _HALO), :].astype(F32) * (t < nt - 1).astype(F32)
    for lc in range(CONV_W // LANES):
        ls = slice(lc * LANES, (lc + 1) * LANES)
        win[lc, 0:_HALO] = top[:, ls]
        win[lc, _HALO:_HALO + tm] = mid[:, ls]
        win[lc, _HALO + tm:] = bot[:, ls]

    for rc in range(tm // _CONV_ROWS):
        for lc in range(CONV_W // LANES):
            ls = slice(lc * LANES, (lc + 1) * LANES)
            acc = jnp.broadcast_to(bdw_ref[0, :, ls], (_CONV_ROWS, LANES))
            for k in range(CONV_K):
                lo = rc * _CONV_ROWS + k + _HALO - CONV_PAD
                acc = acc + win[lc, lo:lo + _CONV_ROWS, :] * wdw_ref[0, k:k + 1, ls]
            cv[rc * _CONV_ROWS:(rc + 1) * _CONV_ROWS, ls] = acc

    c = cv[...]
    mu = jnp.mean(c, axis=-1, keepdims=True)
    cc = c - mu
    var = jnp.mean(cc * cc, axis=-1, keepdims=True)
    ln = cc * lax.rsqrt(var + EPS) * gln_ref[0] + bln_ref[0]
    z = (_silu(ln) * sg_ref[0].astype(F32)).astype(BF16)
    y = _dot(z, wo_ref[0])
    o_ref[0] = x_ref[0] + _mod_row(gate_ref, row) * _rms(y, gpost_ref[0])


def _conv_out(u, sg, x, mod, layer, j, pc, ctx_row, tm):
    b, l, d = x.shape
    per_conv = [pc["w_dw"], pc["b_dw"], pc["g_ln"], pc["b_ln"], pc["w_o"]]
    return pl.pallas_call(
        functools.partial(_conv_out_kernel, ctx_row, tm),
        grid=(b, l // tm),
        in_specs=[pl.BlockSpec((1, l, CONV_W), lambda i, t: (i, 0, 0)), _tok_spec(tm, CONV_W), _tok_spec(tm, d),
                  _mod_spec(layer, 2)] + [_layer_spec(w, j) for w in per_conv] + [_layer_spec(pc["g_post"], layer)],
        out_specs=_tok_spec(tm, d),
        out_shape=jax.ShapeDtypeStruct((b, l, d), F32),
        scratch_shapes=[pltpu.VMEM((CONV_W // LANES, tm + 2 * _HALO, LANES), F32), pltpu.VMEM((tm, CONV_W), F32)],
        compiler_params=_PARAMS,
        name="conv_out",
    )(u, sg, x, mod, *per_conv, pc["g_post"])


def _rope_tables(n_tokens):
    f32 = np.float32
    rows_n = n_tokens // GRID_W
    rows = np.repeat(np.arange(rows_n, dtype=f32), GRID_W)
    cols = np.tile(np.arange(GRID_W, dtype=f32), rows_n)
    half = ROPE_DIM // 2
    freqs = (f32(1.0) / (f32(ROPE_BASE) ** (np.arange(0, half, 2, dtype=f32) / f32(half)))).astype(f32)
    ang_r = rows[:, None] * freqs[None, :]
    ang_c = cols[:, None] * freqs[None, :]
    zero = np.zeros_like(ang_r)
    cos64 = np.concatenate([np.cos(ang_r), np.cos(ang_r), np.cos(ang_c), np.cos(ang_c)], axis=-1)
    s1_64 = np.concatenate([zero, np.sin(ang_r), zero, np.sin(ang_c)], axis=-1)
    s2_64 = np.concatenate([-np.sin(ang_r), zero, -np.sin(ang_c), zero], axis=-1)
    dup = lambda a: jnp.asarray(np.concatenate([a, a], axis=-1).astype(f32))
    return dup(cos64), dup(s1_64), dup(s2_64)


def _mla_params(g_pre, g_post, mla_w_in, mla_g_q, mla_w_uq, mla_g_kv, mla_w_ukv, mla_w_o):
    n = mla_w_in.shape[0]
    kr = mla_w_in[:, :, _KR0:_KR0 + ROPE_DIM]
    w_in = jnp.concatenate([mla_w_in[:, :, :_KR0], kr, kr, mla_w_in[:, :, _KR0 + ROPE_DIM:]], axis=2).astype(BF16)
    w_uq = mla_w_uq.reshape(n, Q_LORA, N_HEADS, NOPE_DIM + ROPE_DIM)
    w_uq = jnp.concatenate([w_uq[..., :NOPE_DIM].reshape(n, Q_LORA, _QN_COLS),
                            w_uq[..., NOPE_DIM:].reshape(n, Q_LORA, _QR_COLS)], axis=2).astype(BF16)
    w_ukv = mla_w_ukv.reshape(n, KV_LORA, N_HEADS, NOPE_DIM + V_DIM)
    w_uk = w_ukv[..., :NOPE_DIM].reshape(n, KV_LORA, _QN_COLS).astype(BF16)
    w_uvt = jnp.swapaxes(w_ukv[..., NOPE_DIM:].reshape(n, KV_LORA, ATTN_W), 1, 2).astype(BF16)
    scale = math.log2(math.e) / math.sqrt(NOPE_DIM + ROPE_DIM)
    return dict(g_pre=g_pre, g_post=g_post, w_in=w_in, g_q=(mla_g_q * scale)[:, None, :], w_uq=w_uq,
                g_kv=mla_g_kv[:, None, :], w_uk=w_uk, w_uvt=w_uvt, w_o=mla_w_o.astype(BF16))


def kernel(x, c, ctx, c_ctx, w_mod, b_mod, g_pre, g_post, mla_w_in, mla_g_q, mla_w_uq, mla_g_kv, mla_w_ukv,
           mla_w_o, cv_w_in, cv_w_dw, cv_b_dw, cv_g_ln, cv_b_ln, cv_w_o):
    b, s, d = x.shape
    lc = ctx.shape[1]
    ctx_row = b
    assert b < MOD_ROWS and s % GRID_W == 0

    cvec = jnp.zeros((MOD_ROWS, d), F32).at[:b].set(c).at[ctx_row].set(c_ctx)
    mod = _modulation(cvec, w_mod, b_mod)

    tables_x = _rope_tables(s)
    g_pre3, g_post3 = g_pre[:, None, :], g_post[:, None, :]
    pm = _mla_params(g_pre3, g_post3, mla_w_in, mla_g_q, mla_w_uq, mla_g_kv, mla_w_ukv, mla_w_o)
    pc = dict(g_pre=g_pre3, g_post=g_post3, w_in=cv_w_in.astype(BF16), w_dw=cv_w_dw.reshape(-1, CONV_K, CONV_W),
              b_dw=cv_b_dw[:, None, :], g_ln=cv_g_ln[:, None, :], b_ln=cv_b_ln[:, None, :], w_o=cv_w_o.astype(BF16))

    conv_inputs = None
    for i in range(DEPTH):
        last = i == DEPTH - 1
        j = i // N_MIXERS
        if i % N_MIXERS == 0:
            next_conv = (pc, (i + 1) // N_MIXERS) if not last and (i + 1) % N_MIXERS == 1 else None
            qn, qr, kn, kr, vt, sg = _mla_proj(x, mod, i, j, pm, tables_x, None, _TM_MLA)
            qn_c, qr_c, kn_c, kr_c, vt_c, sg_c = _mla_proj(ctx, mod, i, j, pm, (), ctx_row, lc)
            og = _attention(qn, qr, sg, [(kn, kr, vt), (kn_c, kr_c, vt_c)], _TQ, _HEADS_X)
            res_x = _out_residual(og, x, mod, i, j, pm, None, _TM_PROJ, next_conv)
            res_c = None
            if not last:
                og_c = _ctx_attention(qn_c, qr_c, sg_c, kn_c, kr_c, vt_c)
                res_c = _out_residual(og_c, ctx, mod, i, j, pm, ctx_row, lc, next_conv)
            if next_conv is None:
                x, ctx = res_x, (ctx if res_c is None else res_c)
            else:
                x, ctx = res_x[0], res_c[0]
                conv_inputs = (res_x[1:], res_c[1:])
        else:
            if conv_inputs is None:
                conv_inputs = (_conv_in(x, mod, i, j, pc, None, _TM_PROJ),
                               _conv_in(ctx, mod, i, j, pc, ctx_row, lc) if not last else None)
            (u, sg), uc_sgc = conv_inputs
            conv_inputs = None
            x = _conv_out(u, sg, x, mod, i, j, pc, None, _TM_CONV)
            if not last:
                ctx = _conv_out(uc_sgc[0], uc_sgc[1], ctx, mod, i, j, pc, ctx_row, lc)
    return x
```

```python
import functools
import math

import jax
import jax.numpy as jnp
import numpy as np
from jax import lax
from jax.experimental import pallas as pl
from jax.experimental.pallas import tpu as pltpu

D_MODEL = 1024
DEPTH = 4
GRID_W = 64
N_MIXERS = 2
N_HEADS = 8
NOPE_DIM = 128
ROPE_DIM = 64
V_DIM = 128
Q_LORA = 384
KV_LORA = 256
ATTN_W = N_HEADS * V_DIM
ROPE_BASE = 10000.0
CONV_W = D_MODEL
CONV_K = 31
CONV_PAD = CONV_K // 2
EPS = 1e-6

LANES = 128
MOD_ROWS = 16
VMEM_LIMIT = 56 * 1024 * 1024
_HEAD_PAIR = 2
_HEADS_X = 2
_ONES_ROWS = 16
_PROB_ROWS = 64
_TM_MLA = 1024
_TM_PROJ = 1024
_TM_CONV = 1024
_TILE_PARTS = 4
_TQ = 1024
_HALO = 16
_CONV_ROWS = 128
assert _HALO >= CONV_PAD + 1

F32 = jnp.float32
BF16 = jnp.bfloat16

_CQ0, _CKV0, _KR0, _GATE0 = 0, Q_LORA, Q_LORA + KV_LORA, Q_LORA + KV_LORA + LANES
_WIN_COLS = _GATE0 + ATTN_W
_QN_COLS = N_HEADS * NOPE_DIM
_QR_COLS = N_HEADS * ROPE_DIM


def _silu(v):
    return v * jax.nn.sigmoid(v)


def _rms(v, g):
    return v * lax.rsqrt(jnp.mean(v * v, axis=-1, keepdims=True) + EPS) * g


def _dot(a, b):
    return jnp.dot(a, b, preferred_element_type=F32)


def _dot_nt(a, b):
    return lax.dot_general(a, b, (((1,), (1,)), ((), ())), preferred_element_type=F32)


def _rope(v, c, s1, s2):
    return v * c + pltpu.roll(v, 16, axis=1) * s1 + pltpu.roll(v, LANES - 16, axis=1) * s2


def _mod_row(ref, row):
    return ref[0, pl.ds(row, 1), :]


def _layer_spec(arr, idx):
    zeros = (0,) * (arr.ndim - 1)
    return pl.BlockSpec((1,) + arr.shape[1:], lambda i, t: (idx,) + zeros, pipeline_mode=pl.Buffered(1))


def _mod_spec(layer, part):
    return pl.BlockSpec((1, MOD_ROWS, D_MODEL), lambda i, t: (layer, 0, part))


def _tok_spec(tm, width):
    return pl.BlockSpec((1, tm, width), lambda i, t: (i, t, 0))


_PARAMS = pltpu.CompilerParams(vmem_limit_bytes=VMEM_LIMIT)


def _mod_kernel(c_ref, w_ref, b_ref, o_ref):
    sc = _silu(c_ref[...]).astype(BF16)
    o_ref[0] = _dot(sc, w_ref[0].astype(BF16)) + b_ref[0]


def _modulation(cvec, w_mod, b_mod):
    d = D_MODEL
    return pl.pallas_call(
        _mod_kernel,
        grid=(DEPTH,),
        in_specs=[
            pl.BlockSpec((MOD_ROWS, d), lambda i: (0, 0)),
            pl.BlockSpec((1, d, 3 * d), lambda i: (i, 0, 0)),
            pl.BlockSpec((1, 1, 3 * d), lambda i: (i, 0, 0)),
        ],
        out_specs=pl.BlockSpec((1, MOD_ROWS, 3 * d), lambda i: (i, 0, 0)),
        out_shape=jax.ShapeDtypeStruct((DEPTH, MOD_ROWS, 3 * d), F32),
        compiler_params=_PARAMS,
        name="modulation",
    )(cvec, w_mod, b_mod.reshape(DEPTH, 1, 3 * d))


def _mla_proj_kernel(ctx_row, x_ref, shift_ref, scale_ref, gpre_ref, win_ref, gq_ref, wuq_ref, gkv_ref,
                     wuk_ref, wuvt_ref, *rest):
    row = pl.program_id(0) if ctx_row is None else ctx_row
    if ctx_row is None:
        tc_ref, ts1_ref, ts2_ref, qn_ref, qr_ref, kn_ref, kr_ref, vt_ref, sg_ref = rest
        tc, ts1, ts2 = tc_ref[...], ts1_ref[...], ts2_ref[...]
        rope = lambda v: _rope(v, tc, ts1, ts2)
    else:
        qn_ref, qr_ref, kn_ref, kr_ref, vt_ref, sg_ref = rest
        rope = lambda v: v
    x = x_ref[0]
    h = _rms(x, gpre_ref[0]) * (1.0 + _mod_row(scale_ref, row)) + _mod_row(shift_ref, row)
    u = _dot(h.astype(BF16), win_ref[0])

    cqn = _rms(u[:, _CQ0:_CKV0], gq_ref[0]).astype(BF16)
    q = _dot(cqn, wuq_ref[0])
    for j in range(_QR_COLS // LANES):
        lo = _QN_COLS + j * LANES
        qr_ref[0, j] = rope(q[:, lo:lo + LANES]).astype(BF16)

    ckvn = _rms(u[:, _CKV0:_KR0], gkv_ref[0]).astype(BF16)
    kn = _dot(ckvn, wuk_ref[0])
    sg = _silu(u[:, _GATE0:])
    for hd in range(N_HEADS):
        hs = slice(hd * NOPE_DIM, (hd + 1) * NOPE_DIM)
        qn_ref[0, hd] = q[:, hs].astype(BF16)
        kn_ref[0, hd] = kn[:, hs].astype(BF16)
        sg_ref[0, hd] = sg[:, hs].astype(BF16)
    vt_ref[0] = _dot_nt(wuvt_ref[0], ckvn).astype(BF16)
    kr_ref[0] = rope(u[:, _KR0:_GATE0]).astype(BF16)


def _mla_proj(x, mod, layer, j, pm, tables, ctx_row, tm):
    b, l, d = x.shape
    assert (ctx_row is None) == (len(tables) == 3)
    tabspec = pl.BlockSpec((tm, LANES), lambda i, t: (t, 0))
    heads = lambda n: pl.BlockSpec((1, n, tm, LANES), lambda i, t: (i, 0, t, 0))
    hshape = lambda n: jax.ShapeDtypeStruct((b, n, l, LANES), BF16)
    nh, npair = N_HEADS, N_HEADS // _HEAD_PAIR
    weights = [pm["w_in"], pm["g_q"], pm["w_uq"], pm["g_kv"], pm["w_uk"], pm["w_uvt"]]
    return pl.pallas_call(
        functools.partial(_mla_proj_kernel, ctx_row),
        grid=(b, l // tm),
        in_specs=[_tok_spec(tm, d), _mod_spec(layer, 0), _mod_spec(layer, 1), _layer_spec(pm["g_pre"], layer)]
                 + [_layer_spec(w, j) for w in weights] + [tabspec] * len(tables),
        out_specs=[heads(nh), heads(npair), heads(nh), _tok_spec(tm, LANES),
                   pl.BlockSpec((1, ATTN_W, tm), lambda i, t: (i, 0, t)), heads(nh)],
        out_shape=[hshape(nh), hshape(npair), hshape(nh), jax.ShapeDtypeStruct((b, l, LANES), BF16),
                   jax.ShapeDtypeStruct((b, ATTN_W, l), BF16), hshape(nh)],
        compiler_params=_PARAMS,
        name="mla_proj",
    )(x, mod, mod, pm["g_pre"], *weights, *tables)


def _attn_kernel(seg_lens, tq, qn_ref, qr_ref, sg_ref, *rest):
    nseg = len(seg_lens)
    seg_refs = rest[:3 * nseg]
    o_ref, k_scr, vt_scr, st_scr, m_scr, p_scr = rest[3 * nseg:]
    lane_half = lax.broadcasted_iota(jnp.int32, (1, LANES), 1) // ROPE_DIM
    hp = qn_ref.shape[1]
    nt = qn_ref.shape[2] // tq
    n_units = hp * nt
    assert n_units >= 2

    for hh in range(hp):
        off = 0
        for s, ls in enumerate(seg_lens):
            kn_ref, kr_ref, vt_ref = seg_refs[3 * s:3 * s + 3]
            k_scr[hh, off:off + ls, :NOPE_DIM] = kn_ref[0, hh]
            in_half = lane_half == hh % _HEAD_PAIR
            k_scr[hh, off:off + ls, NOPE_DIM:] = jnp.where(in_half, kr_ref[0], jnp.zeros((), BF16))
            vt_scr[hh, :V_DIM, off:off + ls] = vt_ref[0, hh * V_DIM:(hh + 1) * V_DIM, :]
            off += ls
        vt_scr[hh, V_DIM:, :] = jnp.ones((_ONES_ROWS, off), BF16)

    def unit(u):
        return u // nt, pl.ds(pl.multiple_of((u % nt) * tq, tq), tq)

    def scores(u):
        hh, rows = unit(u)
        q = jnp.concatenate([qn_ref[0, hh, rows, :], qr_ref[0, hh // _HEAD_PAIR, rows, :]], axis=1)
        st = _dot_nt(k_scr[hh], q)
        st_scr[...] = st
        m_scr[...] = jnp.max(st, axis=0, keepdims=True)

    def probs():
        m = m_scr[...]
        for c in range(0, st_scr.shape[0], _PROB_ROWS):
            rows = slice(c, c + _PROB_ROWS)
            p_scr[rows, :] = jnp.exp2((st_scr[rows, :] - m).astype(BF16))

    def values(u):
        hh, rows = unit(u)
        ot = _dot(vt_scr[hh], p_scr[...])
        ot = ot[:V_DIM] / ot[V_DIM:V_DIM + 1]
        o_ref[0, hh, rows, :] = (ot.T * sg_ref[0, hh, rows, :].astype(F32)).astype(BF16)

    scores(0)
    probs()
    scores(1)

    def iteration(i, carry):
        values(i - 2)
        probs()
        scores(i)
        return carry

    lax.fori_loop(2, n_units, iteration, 0)
    values(n_units - 2)
    probs()
    values(n_units - 1)


def _attention(qn, qr, sg, segments, tq, hp):
    b, nh, lq, _ = qn.shape
    seg_lens = tuple(s[0].shape[2] for s in segments)
    lk = sum(seg_lens)
    assert hp % _HEAD_PAIR == 0 and nh % hp == 0
    pair = lambda l: pl.BlockSpec((1, hp, l, LANES), lambda i, h: (i, h, 0, 0))
    in_specs = [pair(lq), pl.BlockSpec((1, hp // _HEAD_PAIR, lq, LANES), lambda i, h: (i, h, 0, 0)), pair(lq)]
    args = [qn, qr, sg]
    for kn, kr, vt in segments:
        ls = kn.shape[2]
        in_specs += [pair(ls), pl.BlockSpec((1, ls, LANES), lambda i, h: (i, 0, 0)),
                     pl.BlockSpec((1, hp * V_DIM, ls), lambda i, h: (i, h, 0))]
        args += [kn, kr, vt]
    return pl.pallas_call(
        functools.partial(_attn_kernel, seg_lens, tq),
        grid=(b, nh // hp),
        in_specs=in_specs,
        out_specs=pair(lq),
        out_shape=jax.ShapeDtypeStruct((b, nh, lq, LANES), BF16),
        scratch_shapes=[pltpu.VMEM((hp, lk, NOPE_DIM + LANES), BF16),
                        pltpu.VMEM((hp, V_DIM + _ONES_ROWS, lk), BF16),
                        pltpu.VMEM((lk, tq), F32), pltpu.VMEM((1, tq), F32), pltpu.VMEM((lk, tq), BF16)],
        compiler_params=_PARAMS,
        name="attention",
    )(*args)


def _ctx_attn_kernel(qn_ref, qr_ref, sg_ref, kn_ref, kr_ref, vt_ref, o_ref):
    lane_half = lax.broadcasted_iota(jnp.int32, (1, LANES), 1) // ROPE_DIM
    lk = kn_ref.shape[2]
    ones = jnp.ones((_ONES_ROWS, lk), BF16)
    for hd in range(N_HEADS):
        kr = jnp.where(lane_half == hd % _HEAD_PAIR, kr_ref[0], jnp.zeros((), BF16))
        k = jnp.concatenate([kn_ref[0, hd], kr], axis=1)
        q = jnp.concatenate([qn_ref[0, hd], qr_ref[0, hd // _HEAD_PAIR]], axis=1)
        st = _dot_nt(k, q)
        p = jnp.exp2((st - jnp.max(st, axis=0, keepdims=True)).astype(BF16))
        vte = jnp.concatenate([vt_ref[0, hd * V_DIM:(hd + 1) * V_DIM, :], ones], axis=0)
        ot = _dot(vte, p)
        ot = ot[:V_DIM] / ot[V_DIM:V_DIM + 1]
        o_ref[0, hd] = (ot.T * sg_ref[0, hd].astype(F32)).astype(BF16)


def _ctx_attention(qn, qr, sg, kn, kr, vt):
    b, nh, l, _ = qn.shape
    heads = lambda n: pl.BlockSpec((1, n, l, LANES), lambda i: (i, 0, 0, 0))
    return pl.pallas_call(
        _ctx_attn_kernel,
        grid=(b,),
        in_specs=[heads(nh), heads(nh // _HEAD_PAIR), heads(nh), heads(nh),
                  pl.BlockSpec((1, l, LANES), lambda i: (i, 0, 0)), pl.BlockSpec((1, nh * V_DIM, l), lambda i: (i, 0, 0))],
        out_specs=heads(nh),
        out_shape=jax.ShapeDtypeStruct((b, nh, l, LANES), BF16),
        compiler_params=_PARAMS,
        name="ctx_attention",
    )(qn, qr, sg, kn, kr, vt)


def _conv_in_math(x, row, rows, shift_ref, scale_ref, gpre_ref, win_ref, u_ref, sg_ref):
    h = _rms(x, gpre_ref[0]) * (1.0 + _mod_row(scale_ref, row)) + _mod_row(shift_ref, row)
    u3 = _dot(h.astype(BF16), win_ref[0])
    u_ref[0, rows, :] = (u3[:, :CONV_W] * jax.nn.sigmoid(u3[:, CONV_W:2 * CONV_W])).astype(BF16)
    sg_ref[0, rows, :] = _silu(u3[:, 2 * CONV_W:]).astype(BF16)


def _row_parts(tm):
    n = _TILE_PARTS if tm % (_TILE_PARTS * LANES) == 0 else 1
    return [slice(i * (tm // n), (i + 1) * (tm // n)) for i in range(n)]


def _out_kernel(ctx_row, fuse_conv_in, og_ref, x_ref, gate_ref, wo_ref, gpost_ref, *rest):
    row = pl.program_id(0) if ctx_row is None else ctx_row
    parts = _row_parts(x_ref.shape[1])
    ys = []
    for rows in parts:
        og = jnp.concatenate([og_ref[0, hd, rows, :] for hd in range(og_ref.shape[1])], axis=1)
        ys.append(_dot(og, wo_ref[0]))
    for rows, y in zip(parts, ys):
        x_new = x_ref[0, rows, :] + _mod_row(gate_ref, row) * _rms(y, gpost_ref[0])
        if fuse_conv_in:
            shift_ref, scale_ref, gpre_ref, win_ref, o_ref, u_ref, sg_ref = rest
            _conv_in_math(x_new, row, rows, shift_ref, scale_ref, gpre_ref, win_ref, u_ref, sg_ref)
        else:
            o_ref, = rest
        o_ref[0, rows, :] = x_new


def _out_residual(og, x, mod, layer, j, pm, ctx_row, tm, next_conv=None):
    b, l, d = x.shape
    in_specs = [pl.BlockSpec((1, og.shape[1], tm, LANES), lambda i, t: (i, 0, t, 0)), _tok_spec(tm, d),
                _mod_spec(layer, 2), _layer_spec(pm["w_o"], j), _layer_spec(pm["g_post"], layer)]
    args = [og, x, mod, pm["w_o"], pm["g_post"]]
    out_specs, out_shape = [_tok_spec(tm, d)], [jax.ShapeDtypeStruct((b, l, d), F32)]
    if next_conv is not None:
        pc, jc = next_conv
        in_specs += [_mod_spec(layer + 1, 0), _mod_spec(layer + 1, 1), _layer_spec(pc["g_pre"], layer + 1),
                     _layer_spec(pc["w_in"], jc)]
        args += [mod, mod, pc["g_pre"], pc["w_in"]]
        out_specs += [_tok_spec(tm, CONV_W)] * 2
        out_shape += [jax.ShapeDtypeStruct((b, l, CONV_W), BF16)] * 2
    res = pl.pallas_call(
        functools.partial(_out_kernel, ctx_row, next_conv is not None),
        grid=(b, l // tm),
        in_specs=in_specs,
        out_specs=out_specs,
        out_shape=out_shape,
        compiler_params=_PARAMS,
        name="out_residual",
    )(*args)
    return res if next_conv is not None else res[0]


def _conv_in_kernel(ctx_row, x_ref, shift_ref, scale_ref, gpre_ref, win_ref, u_ref, sg_ref):
    row = pl.program_id(0) if ctx_row is None else ctx_row
    for rows in _row_parts(x_ref.shape[1]):
        _conv_in_math(x_ref[0, rows, :], row, rows, shift_ref, scale_ref, gpre_ref, win_ref, u_ref, sg_ref)


def _conv_in(x, mod, layer, j, pc, ctx_row, tm):
    b, l, d = x.shape
    return pl.pallas_call(
        functools.partial(_conv_in_kernel, ctx_row),
        grid=(b, l // tm),
        in_specs=[_tok_spec(tm, d), _mod_spec(layer, 0), _mod_spec(layer, 1), _layer_spec(pc["g_pre"], layer),
                  _layer_spec(pc["w_in"], j)],
        out_specs=[_tok_spec(tm, CONV_W)] * 2,
        out_shape=[jax.ShapeDtypeStruct((b, l, CONV_W), BF16)] * 2,
        compiler_params=_PARAMS,
        name="conv_in",
    )(x, mod, mod, pc["g_pre"], pc["w_in"])


def _conv_out_kernel(ctx_row, tm, u_ref, sg_ref, x_ref, gate_ref, wdw_ref, bdw_ref, gln_ref, bln_ref,
                     wo_ref, gpost_ref, o_ref, win, cv):
    row = pl.program_id(0) if ctx_row is None else ctx_row
    t = pl.program_id(1)
    nt = pl.num_programs(1)
    l = u_ref.shape[1]
    r0 = pl.multiple_of(t * tm, tm)

    top0 = pl.multiple_of(jnp.maximum(r0 - _HALO, 0), _HALO)
    bot0 = pl.multiple_of(jnp.minimum(r0 + tm, l - _HALO), _HALO)
    top = u_ref[0, pl.ds(top0, _HALO), :].astype(F32) * (t > 0).astype(F32)
    mid = u_ref[0, pl.ds(r0, tm), :].astype(F32)
    bot = u_ref[0, pl.ds(bot0, _HALO), :].astype(F32) * (t < nt - 1).astype(F32)
    for lc in range(CONV_W // LANES):
        ls = slice(lc * LANES, (lc + 1) * LANES)
        win[lc, 0:_HALO] = top[:, ls]
        win[lc, _HALO:_HALO + tm] = mid[:, ls]
        win[lc, _HALO + tm:] = bot[:, ls]

    for rc in range(tm // _CONV_ROWS):
        for lc in range(CONV_W // LANES):
            ls = slice(lc * LANES, (lc + 1) * LANES)
            acc = jnp.broadcast_to(bdw_ref[0, :, ls], (_CONV_ROWS, LANES))
            for k in range(CONV_K):
                lo = rc * _CONV_ROWS + k + _HALO - CONV_PAD
                acc = acc + win[lc, lo:lo + _CONV_ROWS, :] * wdw_ref[0, k:k + 1, ls]
            cv[rc * _CONV_ROWS:(rc + 1) * _CONV_ROWS, ls] = acc

    c = cv[...]
    mu = jnp.mean(c, axis=-1, keepdims=True)
    cc = c - mu
    var = jnp.mean(cc * cc, axis=-1, keepdims=True)
    ln = cc * lax.rsqrt(var + EPS) * gln_ref[0] + bln_ref[0]
    z = (_silu(ln) * sg_ref[0].astype(F32)).astype(BF16)
    y = _dot(z, wo_ref[0])
    o_ref[0] = x_ref[0] + _mod_row(gate_ref, row) * _rms(y, gpost_ref[0])


def _conv_out(u, sg, x, mod, layer, j, pc, ctx_row, tm):
    b, l, d = x.shape
    per_conv = [pc["w_dw"], pc["b_dw"], pc["g_ln"], pc["b_ln"], pc["w_o"]]
    return pl.pallas_call(
        functools.partial(_conv_out_kernel, ctx_row, tm),
        grid=(b, l // tm),
        in_specs=[pl.BlockSpec((1, l, CONV_W), lambda i, t: (i, 0, 0)), _tok_spec(tm, CONV_W), _tok_spec(tm, d),
                  _mod_spec(layer, 2)] + [_layer_spec(w, j) for w in per_conv] + [_layer_spec(pc["g_post"], layer)],
        out_specs=_tok_spec(tm, d),
        out_shape=jax.ShapeDtypeStruct((b, l, d), F32),
        scratch_shapes=[pltpu.VMEM((CONV_W // LANES, tm + 2 * _HALO, LANES), F32), pltpu.VMEM((tm, CONV_W), F32)],
        compiler_params=_PARAMS,
        name="conv_out",
    )(u, sg, x, mod, *per_conv, pc["g_post"])


def _rope_tables(n_tokens):
    f32 = np.float32
    rows_n = n_tokens // GRID_W
    rows = np.repeat(np.arange(rows_n, dtype=f32), GRID_W)
    cols = np.tile(np.arange(GRID_W, dtype=f32), rows_n)
    half = ROPE_DIM // 2
    freqs = (f32(1.0) / (f32(ROPE_BASE) ** (np.arange(0, half, 2, dtype=f32) / f32(half)))).astype(f32)
    ang_r = rows[:, None] * freqs[None, :]
    ang_c = cols[:, None] * freqs[None, :]
    zero = np.zeros_like(ang_r)
    cos64 = np.concatenate([np.cos(ang_r), np.cos(ang_r), np.cos(ang_c), np.cos(ang_c)], axis=-1)
    s1_64 = np.concatenate([zero, np.sin(ang_r), zero, np.sin(ang_c)], axis=-1)
    s2_64 = np.concatenate([-np.sin(ang_r), zero, -np.sin(ang_c), zero], axis=-1)
    dup = lambda a: jnp.asarray(np.concatenate([a, a], axis=-1).astype(f32))
    return dup(cos64), dup(s1_64), dup(s2_64)


def _mla_params(g_pre, g_post, mla_w_in, mla_g_q, mla_w_uq, mla_g_kv, mla_w_ukv, mla_w_o):
    n = mla_w_in.shape[0]
    kr = mla_w_in[:, :, _KR0:_KR0 + ROPE_DIM]
    w_in = jnp.concatenate([mla_w_in[:, :, :_KR0], kr, kr, mla_w_in[:, :, _KR0 + ROPE_DIM:]], axis=2).astype(BF16)
    w_uq = mla_w_uq.reshape(n, Q_LORA, N_HEADS, NOPE_DIM + ROPE_DIM)
    w_uq = jnp.concatenate([w_uq[..., :NOPE_DIM].reshape(n, Q_LORA, _QN_COLS),
                            w_uq[..., NOPE_DIM:].reshape(n, Q_LORA, _QR_COLS)], axis=2).astype(BF16)
    w_ukv = mla_w_ukv.reshape(n, KV_LORA, N_HEADS, NOPE_DIM + V_DIM)
    w_uk = w_ukv[..., :NOPE_DIM].reshape(n, KV_LORA, _QN_COLS).astype(BF16)
    w_uvt = jnp.swapaxes(w_ukv[..., NOPE_DIM:].reshape(n, KV_LORA, ATTN_W), 1, 2).astype(BF16)
    scale = math.log2(math.e) / math.sqrt(NOPE_DIM + ROPE_DIM)
    return dict(g_pre=g_pre, g_post=g_post, w_in=w_in, g_q=(mla_g_q * scale)[:, None, :], w_uq=w_uq,
                g_kv=mla_g_kv[:, None, :], w_uk=w_uk, w_uvt=w_uvt, w_o=mla_w_o.astype(BF16))


def kernel(x, c, ctx, c_ctx, w_mod, b_mod, g_pre, g_post, mla_w_in, mla_g_q, mla_w_uq, mla_g_kv, mla_w_ukv,
           mla_w_o, cv_w_in, cv_w_dw, cv_b_dw, cv_g_ln, cv_b_ln, cv_w_o):
    b, s, d = x.shape
    lc = ctx.shape[1]
    ctx_row = b
    assert b < MOD_ROWS and s % GRID_W == 0

    cvec = jnp.zeros((MOD_ROWS, d), F32).at[:b].set(c).at[ctx_row].set(c_ctx)
    mod = _modulation(cvec, w_mod, b_mod)

    tables_x = _rope_tables(s)
    g_pre3, g_post3 = g_pre[:, None, :], g_post[:, None, :]
    pm = _mla_params(g_pre3, g_post3, mla_w_in, mla_g_q, mla_w_uq, mla_g_kv, mla_w_ukv, mla_w_o)
    pc = dict(g_pre=g_pre3, g_post=g_post3, w_in=cv_w_in.astype(BF16), w_dw=cv_w_dw.reshape(-1, CONV_K, CONV_W),
              b_dw=cv_b_dw[:, None, :], g_ln=cv_g_ln[:, None, :], b_ln=cv_b_ln[:, None, :], w_o=cv_w_o.astype(BF16))

    conv_inputs = None
    for i in range(DEPTH):
        last = i == DEPTH - 1
        j = i // N_MIXERS
        if i % N_MIXERS == 0:
            next_conv = (pc, (i + 1) // N_MIXERS) if not last and (i + 1) % N_MIXERS == 1 else None
            qn, qr, kn, kr, vt, sg = _mla_proj(x, mod, i, j, pm, tables_x, None, _TM_MLA)
            qn_c, qr_c, kn_c, kr_c, vt_c, sg_c = _mla_proj(ctx, mod, i, j, pm, (), ctx_row, lc)
            og = _attention(qn, qr, sg, [(kn, kr, vt), (kn_c, kr_c, vt_c)], _TQ, _HEADS_X)
            res_x = _out_residual(og, x, mod, i, j, pm, None, _TM_PROJ, next_conv)
            res_c = None
            if not last:
                og_c = _ctx_attention(qn_c, qr_c, sg_c, kn_c, kr_c, vt_c)
                res_c = _out_residual(og_c, ctx, mod, i, j, pm, ctx_row, lc, next_conv)
            if next_conv is None:
                x, ctx = res_x, (ctx if res_c is None else res_c)
            else:
                x, ctx = res_x[0], res_c[0]
                conv_inputs = (res_x[1:], res_c[1:])
        else:
            if conv_inputs is None:
                conv_inputs = (_conv_in(x, mod, i, j, pc, None, _TM_PROJ),
                               _conv_in(ctx, mod, i, j, pc, ctx_row, lc) if not last else None)
            (u, sg), uc_sgc = conv_inputs
            conv_inputs = None
            x = _conv_out(u, sg, x, mod, i, j, pc, None, _TM_CONV)
            if not last:
                ctx = _conv_out(uc_sgc[0], uc_sgc[1], ctx, mod, i, j, pc, ctx_row, lc)
    return x
```
